```python
import jax
import jax.numpy as jnp
from jax import lax
import numpy as np

D_MODEL = 1024
BATCH = 8
SEQ = 2048
DEPTH = 4

CHUNK = 64
Q_BLOCK = 128
N_HEADS = 4
HEAD_DIM = 128
BRANCH_W = N_HEADS * HEAD_DIM
RET_DK = HEAD_DIM
RET_DV = HEAD_DIM
GLA_DK = HEAD_DIM // 2
GLA_DV = HEAD_DIM
GLA_LOWRANK = 16
GLA_TAU = 16.0
FOX_D = HEAD_DIM
N_BRANCH = 3
D_FF = 2816
CONV_W = 3
ROPE_BASE = 10000.0
EPS = 1e-6

IN_SPLITS = (
    N_HEADS * RET_DK, N_HEADS * RET_DK, N_HEADS * RET_DV, N_HEADS * RET_DV,
    N_HEADS * GLA_DK, N_HEADS * GLA_DK, N_HEADS * GLA_DV, GLA_LOWRANK, N_HEADS * GLA_DV,
    N_HEADS * FOX_D, N_HEADS * FOX_D, N_HEADS * FOX_D, N_HEADS,
)
IN_W = sum(IN_SPLITS)

kernel_name = 'hybrid_ret_gla_fox_adaln_convffn'

F32 = jnp.float32


def rms_norm(x, g):
    xf = x.astype(F32)
    y = xf * lax.rsqrt(jnp.mean(xf * xf, axis=-1, keepdims=True) + EPS)
    return (y * g).astype(x.dtype)


def head_group_norm(x, g):
    xf = x.astype(F32)
    mu = jnp.mean(xf, axis=-1, keepdims=True)
    xc = xf - mu
    return xc * lax.rsqrt(jnp.mean(xc * xc, axis=-1, keepdims=True) + EPS) * g


def modulate(h, shift, scale):
    return h * (1.0 + scale[:, None, :]) + shift[:, None, :]


def rotary(x, pos):
    half = x.shape[-1] // 2
    inv_freq = ROPE_BASE ** (-jnp.arange(half, dtype=F32) / half)
    ang = pos[:, None] * inv_freq[None, :]
    cos = jnp.cos(ang)[None, :, None, :]
    sin = jnp.sin(ang)[None, :, None, :]
    x1, x2 = x[..., :half], x[..., half:]
    return jnp.concatenate([x1 * cos - x2 * sin, x1 * sin + x2 * cos], axis=-1)


def retention(q, k, v):
    b, s, h, dk = q.shape
    dv = v.shape[-1]
    n = s // CHUNK
    pos = jnp.arange(s, dtype=F32)
    q = rotary(q.astype(F32), pos)
    k = rotary(k.astype(F32), pos) * dk ** -0.5
    v = v.astype(F32)
    log_g = jnp.log1p(-jnp.exp2(-5.0 - jnp.arange(h, dtype=F32)))
    idx = jnp.arange(CHUNK, dtype=F32)
    d_intra = jnp.exp(jnp.abs(idx[:, None] - idx[None, :])[None] * log_g[:, None, None])
    qc = q.reshape(b, n, CHUNK, h, dk)
    kc = k.reshape(b, n, CHUNK, h, dk)
    vc = v.reshape(b, n, CHUNK, h, dv)
    scores = jnp.einsum('bnihd,bnjhd->bnhij', qc, kc) * d_intra
    o_intra = jnp.einsum('bnhij,bnjhe->bnihe', scores, vc)
    k_w = jnp.exp((CHUNK - 1.0 - idx)[:, None] * log_g[None, :])
    kv = jnp.einsum('bnjhd,bnjhe->nbhde', kc * k_w[:, :, None], vc)
    g_chunk = jnp.exp(CHUNK * log_g)[None, :, None, None]

    def step(r, kv_n):
        return g_chunk * r + kv_n, r

    _, r_prev = lax.scan(step, jnp.zeros((b, h, dk, dv), F32), kv)
    q_w = jnp.exp((idx + 1.0)[:, None] * log_g[None, :])
    o_cross = jnp.einsum('bnihd,nbhde->bnihe', qc * q_w[:, :, None], r_prev)
    return (o_intra + o_cross).reshape(b, s, h, dv)


def gla(q, k, v, log_a):
    b, s, h, dk = q.shape
    dv = v.shape[-1]
    n = s // CHUNK
    qc = (q.astype(F32) * dk ** -0.5).reshape(b, n, CHUNK, h, dk)
    kc = k.astype(F32).reshape(b, n, CHUNK, h, dk)
    vc = v.astype(F32).reshape(b, n, CHUNK, h, dv)
    la = log_a.astype(F32).reshape(b, n, CHUNK, h, dk)
    b_cum = jnp.cumsum(la, axis=2)
    b_end = b_cum[:, :, -1:]
    kv = jnp.einsum('bnjhd,bnjhe->nbhde', kc * jnp.exp(b_end - b_cum), vc)
    a = jnp.exp(b_end[:, :, 0]).transpose(1, 0, 2, 3)

    def step(st, inp):
        a_n, kv_n = inp
        st = a_n[..., None] * st + kv_n
        return st, st

    _, s_all = lax.scan(step, jnp.zeros((b, h, dk, dv), F32), (a, kv))
    o = jnp.einsum('bnihd,nbhde->bnihe', qc, s_all)
    return o.reshape(b, s, h, dv)


def forgetting_attention(q, k, v, f_logit):
    b, s, h, d = q.shape
    log_f = jax.nn.log_sigmoid(f_logit.astype(F32))
    cum = jnp.cumsum(log_f, axis=1).transpose(0, 2, 1)
    outs = []
    for blk in range(s // Q_BLOCK):
        q0 = blk * Q_BLOCK
        q1 = q0 + Q_BLOCK
        logits = jnp.einsum('bihd,bjhd->bhij', q[:, q0:q1], k[:, :q1]).astype(F32) * d ** -0.5
        logits = logits + cum[:, :, q0:q1, None] - cum[:, :, None, :q1]
        mask = jnp.arange(q0, q1)[:, None] >= jnp.arange(q1)[None, :]
        p = jax.nn.softmax(jnp.where(mask, logits, -jnp.inf), axis=-1).astype(v.dtype)
        outs.append(jnp.einsum('bhij,bjhe->bihe', p, v[:, :q1]))
    return jnp.concatenate(outs, axis=1)


def causal_dwconv(u, w, bias):
    s = u.shape[1]
    up = jnp.pad(u, ((0, 0), (CONV_W - 1, 0), (0, 0)))
    out = bias
    for j in range(CONV_W):
        out = out + w[j] * up[:, j:j + s]
    return out


def hybrid_layer(x, c_act, norm1_g, norm2_g, w_ada, b_ada, w_in, w_gla_a2, b_gla_a, b_fox_f,
                 ret_norm_g, gla_norm_g, q_norm_g, k_norm_g, w_br, w_mg, b_mg, w_o,
                 w_up, w_conv, b_conv, w_down):
    b, s = x.shape[0], x.shape[1]
    mod = c_act @ w_ada + b_ada
    shift1, scale1, gate1, shift2, scale2, gate2 = jnp.split(mod, 6, axis=-1)

    h = modulate(rms_norm(x, norm1_g), shift1, scale1)
    split_points = np.cumsum(IN_SPLITS)[:-1].tolist()
    (rq, rk, rv, rg, gq, gk, gv, glr, gg, fq, fk, fv, ff) = jnp.split(h @ w_in, split_points, axis=-1)

    def heads(t, d):
        return t.reshape(b, s, N_HEADS, d)

    ret = retention(heads(rq, RET_DK), heads(rk, RET_DK), heads(rv, RET_DV))
    ret = head_group_norm(ret, ret_norm_g.reshape(N_HEADS, RET_DV)).reshape(b, s, BRANCH_W)
    ret = jax.nn.silu(rg) * ret

    log_a = jax.nn.log_sigmoid((glr @ w_gla_a2 + b_gla_a).astype(F32)) / GLA_TAU
    gla_o = gla(heads(gq, GLA_DK), heads(gk, GLA_DK), heads(gv, GLA_DV), heads(log_a, GLA_DK))
    gla_o = rms_norm(gla_o, gla_norm_g).reshape(b, s, BRANCH_W)
    gla_o = jax.nn.silu(gg) * gla_o

    fox_o = forgetting_attention(rms_norm(heads(fq, FOX_D), q_norm_g),
                                 rms_norm(heads(fk, FOX_D), k_norm_g),
                                 heads(fv, FOX_D), ff + b_fox_f).reshape(b, s, BRANCH_W)

    branches = jnp.stack([ret, gla_o, fox_o], axis=2).astype(h.dtype)
    y_br = jnp.einsum('bsnw,nwd->bsnd', branches, w_br)
    gates = jax.nn.sigmoid(h @ w_mg + b_mg).reshape(b, s, N_BRANCH, D_MODEL)
    mixed = jnp.sum(gates * y_br, axis=2) @ w_o
    x = x + (gate1[:, None, :] * mixed).astype(x.dtype)

    h2 = modulate(rms_norm(x, norm2_g), shift2, scale2)
    u, g = jnp.split(h2 @ w_up, 2, axis=-1)
    u = causal_dwconv(u, w_conv, b_conv)
    y = (jax.nn.silu(u) * g) @ w_down
    x = x + (gate2[:, None, :] * y).astype(x.dtype)
    return x


def setup_inputs(seed: int = 0) -> dict:
    key = jax.random.key(seed)
    ks = jax.random.split(key, 24)

    def nrm(k, shape, scale):
        return jax.random.normal(k, shape, F32) * scale

    L, D = DEPTH, D_MODEL
    return {
        'x': nrm(ks[0], (BATCH, SEQ, D), 1.0),
        'c': nrm(ks[1], (BATCH, D), 1.0),
        'norm1_g': 1.0 + nrm(ks[2], (L, D), 0.02),
        'norm2_g': 1.0 + nrm(ks[3], (L, D), 0.02),
        'w_ada': nrm(ks[4], (L, D, 6 * D), 0.5 * D ** -0.5),
        'b_ada': nrm(ks[5], (L, 6 * D), 0.02),
        'w_in': nrm(ks[6], (L, D, IN_W), D ** -0.5),
        'w_gla_a2': nrm(ks[7], (L, GLA_LOWRANK, N_HEADS * GLA_DK), GLA_LOWRANK ** -0.5),
        'b_gla_a': nrm(ks[8], (L, N_HEADS * GLA_DK), 0.1),
        'b_fox_f': 1.0 + nrm(ks[9], (L, N_HEADS), 0.1),
        'ret_norm_g': 1.0 + nrm(ks[10], (L, N_HEADS * RET_DV), 0.02),
        'gla_norm_g': 1.0 + nrm(ks[11], (L, GLA_DV), 0.02),
        'q_norm_g': 1.0 + nrm(ks[12], (L, FOX_D), 0.02),
        'k_norm_g': 1.0 + nrm(ks[13], (L, FOX_D), 0.02),
        'w_br': nrm(ks[14], (L, N_BRANCH, BRANCH_W, D), BRANCH_W ** -0.5),
        'w_mg': nrm(ks[15], (L, D, N_BRANCH * D), D ** -0.5),
        'b_mg': nrm(ks[16], (L, N_BRANCH * D), 0.02),
        'w_o': nrm(ks[17], (L, D, D), D ** -0.5),
        'w_up': nrm(ks[18], (L, D, 2 * D_FF), D ** -0.5),
        'w_conv': nrm(ks[19], (L, CONV_W, D_FF), CONV_W ** -0.5),
        'b_conv': nrm(ks[20], (L, D_FF), 0.02),
        'w_down': nrm(ks[21], (L, D_FF, D), D_FF ** -0.5),
    }


def reference(x, c, norm1_g, norm2_g, w_ada, b_ada, w_in, w_gla_a2, b_gla_a, b_fox_f,
              ret_norm_g, gla_norm_g, q_norm_g, k_norm_g, w_br, w_mg, b_mg, w_o,
              w_up, w_conv, b_conv, w_down):
    c_act = jax.nn.silu(c)
    for l in range(DEPTH):
        x = hybrid_layer(x, c_act, norm1_g[l], norm2_g[l], w_ada[l], b_ada[l], w_in[l],
                         w_gla_a2[l], b_gla_a[l], b_fox_f[l], ret_norm_g[l], gla_norm_g[l],
                         q_norm_g[l], k_norm_g[l], w_br[l], w_mg[l], b_mg[l], w_o[l],
                         w_up[l], w_conv[l], b_conv[l], w_down[l])
    return x
```

```python
import functools

import jax
import jax.numpy as jnp
from jax import lax
from jax.experimental import pallas as pl
from jax.experimental.pallas import tpu as pltpu

F32 = jnp.float32
BF16 = jnp.bfloat16

CHUNK = 64
N_HEADS = 4
HEAD_DIM = 128
BRANCH_W = N_HEADS * HEAD_DIM
GLA_DK = HEAD_DIM // 2
GLA_LOWRANK = 16
GLA_TAU = 16.0
N_BRANCH = 3
CONV_W = 3
ROPE_BASE = 10000.0
EPS = 1e-6

LANES = 128
FF_LANE0 = GLA_LOWRANK

VMEM_LIMIT = 56 * 1024 * 1024

TS_IN = 256
TS_MIX = 256
TQ_FOX = 256
TS_OUT = 256
TS_FFN = 256


def _dot(a, b):
    return jnp.dot(a, b, preferred_element_type=F32)


def _dot_nt(a, b):
    return lax.dot_general(a, b, (((1,), (1,)), ((), ())), preferred_element_type=F32)


def _dot_tn(a, b):
    return lax.dot_general(a, b, (((0,), (0,)), ((), ())), preferred_element_type=F32)


def _split3_dot(m_bf16, x):
    x_hi = x.astype(BF16)
    r1 = x - x_hi.astype(F32)
    x_mid = r1.astype(BF16)
    x_lo = (r1 - x_mid.astype(F32)).astype(BF16)
    return _dot(m_bf16, x_hi) + _dot(m_bf16, x_mid) + _dot(m_bf16, x_lo)


def _log_sigmoid(x):
    return jnp.minimum(x, 0.0) - jnp.log1p(jnp.exp(-jnp.abs(x)))


def _sigmoid(x):
    return 1.0 / (1.0 + jnp.exp(-x))


def _silu(x):
    return x * _sigmoid(x)


def _norm_modulate(x, g, shift, scale):
    ms = jnp.mean(x * x, axis=-1, keepdims=True)
    y = x * lax.rsqrt(ms + EPS) * g
    return y * (1.0 + scale) + shift


def _params(*sem):
    return pltpu.CompilerParams(dimension_semantics=sem, vmem_limit_bytes=VMEM_LIMIT)


def _ada_kernel(c_ref, w_ref, b_ref, o_ref):
    c = c_ref[...]
    c_act = _silu(c).astype(BF16)
    o_ref[0] = _dot(c_act, w_ref[0].astype(BF16)) + b_ref[0]


def _ada_call(c, w_ada, b_ada):
    depth, d, n = w_ada.shape
    b = c.shape[0]
    tn = n // 4
    return pl.pallas_call(
        _ada_kernel,
        out_shape=jax.ShapeDtypeStruct((depth, b, n), F32),
        grid=(depth, n // tn),
        in_specs=[
            pl.BlockSpec((b, d), lambda l, j: (0, 0)),
            pl.BlockSpec((1, d, tn), lambda l, j: (l, 0, j)),
            pl.BlockSpec((1, 1, tn), lambda l, j: (l, 0, j)),
        ],
        out_specs=pl.BlockSpec((1, b, tn), lambda l, j: (l, 0, j)),
        compiler_params=_params("arbitrary", "arbitrary"),
        name="ada_mod",
    )(c, w_ada, b_ada.reshape(depth, 1, n))


def _inproj_kernel(x_ref, mod_ref, g1_ref, wmain_ref, wsmall_ref, wa2_ref, ba_ref, bfox_ref,
                   cos_ref, sin_ref, qg_ref, kg_ref, tri_ref,
                   r_ref, gl_ref, f_ref, la_ref, cum_ref, carry_ref):
    d = x_ref.shape[2]
    w = BRANCH_W

    @pl.when(pl.program_id(1) == 0)
    def _():
        carry_ref[...] = jnp.zeros_like(carry_ref)

    x = x_ref[0]
    mod = mod_ref[0]
    h = _norm_modulate(x, g1_ref[...], mod[:, 0:d], mod[:, d:2 * d])
    hb = h.astype(BF16)
    cos = cos_ref[...]
    sin = sin_ref[...]

    def heads_map(acc, fn):
        return jnp.concatenate(
            [fn(acc[:, i * HEAD_DIM:(i + 1) * HEAD_DIM]) for i in range(N_HEADS)], axis=1)

    def rope(t):
        return t * cos + pltpu.roll(t, HEAD_DIM // 2, axis=1) * sin

    def qk_norm(g):
        def fn(t):
            ms = jnp.mean(t * t, axis=-1, keepdims=True)
            return t * lax.rsqrt(ms + EPS) * g
        return fn

    def proj(c0, n):
        return _dot(hb, wmain_ref[:, c0:c0 + n])

    r_ref[0, :, 0:w] = heads_map(proj(0, w), rope).astype(BF16)
    r_ref[0, :, w:2 * w] = (heads_map(proj(w, w), rope) * HEAD_DIM ** -0.5).astype(BF16)
    r_ref[0, :, 2 * w:4 * w] = proj(2 * w, 2 * w).astype(BF16)
    c0 = 4 * w
    gqk = proj(c0, w)
    lane = lax.broadcasted_iota(jnp.int32, gqk.shape, 1)
    gqk = jnp.where(lane < N_HEADS * GLA_DK, gqk * GLA_DK ** -0.5, gqk)
    gl_ref[0, :, 0:w] = gqk.astype(BF16)
    gl_ref[0, :, w:3 * w] = proj(c0 + w, 2 * w).astype(BF16)
    c0 = 7 * w
    f_ref[0, :, 0:w] = (heads_map(proj(c0, w), qk_norm(qg_ref[...])) * HEAD_DIM ** -0.5).astype(BF16)
    f_ref[0, :, w:2 * w] = heads_map(proj(c0 + w, w), qk_norm(kg_ref[...])).astype(BF16)
    f_ref[0, :, 2 * w:3 * w] = proj(c0 + 2 * w, w).astype(BF16)

    small = _dot(hb, wsmall_ref[...])
    la_pre = _dot(small.astype(BF16), wa2_ref[...]) + ba_ref[...]
    la_ref[0] = _log_sigmoid(la_pre) * (1.0 / GLA_TAU)
    lane = lax.broadcasted_iota(jnp.int32, small.shape, 1)
    is_ff = (lane >= FF_LANE0) & (lane < FF_LANE0 + N_HEADS)
    log_f = jnp.where(is_ff, _log_sigmoid(small + bfox_ref[...]), 0.0)
    cum = _split3_dot(tri_ref[...], log_f) + carry_ref[...]
    cum_ref[0] = cum
    carry_ref[...] = cum[cum.shape[0] - 1:cum.shape[0], :]


def _inproj_call(x, mod3, g1, wmain, wsmall, wa2, ba, bfox, cos_t, sin_t, qg, kg, tri):
    b, s, d = x.shape
    ts = min(TS_IN, s)
    w = BRANCH_W
    const2 = lambda i, j: (0, 0)
    tile = lambda i, j: (i, j, 0)
    return pl.pallas_call(
        _inproj_kernel,
        out_shape=(
            jax.ShapeDtypeStruct((b, s, 4 * w), BF16),
            jax.ShapeDtypeStruct((b, s, 3 * w), BF16),
            jax.ShapeDtypeStruct((b, s, 3 * w), BF16),
            jax.ShapeDtypeStruct((b, s, N_HEADS * GLA_DK), F32),
            jax.ShapeDtypeStruct((b, s, LANES), F32),
        ),
        grid=(b, s // ts),
        in_specs=[
            pl.BlockSpec((1, ts, d), tile),
            pl.BlockSpec((1, 1, mod3.shape[2]), lambda i, j: (i, 0, 0)),
            pl.BlockSpec(g1.shape, const2),
            pl.BlockSpec(wmain.shape, const2),
            pl.BlockSpec(wsmall.shape, const2),
            pl.BlockSpec(wa2.shape, const2),
            pl.BlockSpec(ba.shape, const2),
            pl.BlockSpec(bfox.shape, const2),
            pl.BlockSpec((ts, HEAD_DIM), lambda i, j: (j, 0)),
            pl.BlockSpec((ts, HEAD_DIM), lambda i, j: (j, 0)),
            pl.BlockSpec(qg.shape, const2),
            pl.BlockSpec(kg.shape, const2),
            pl.BlockSpec(tri.shape, const2),
        ],
        out_specs=(
            pl.BlockSpec((1, ts, 4 * w), tile),
            pl.BlockSpec((1, ts, 3 * w), tile),
            pl.BlockSpec((1, ts, 3 * w), tile),
            pl.BlockSpec((1, ts, N_HEADS * GLA_DK), tile),
            pl.BlockSpec((1, ts, LANES), tile),
        ),
        scratch_shapes=[pltpu.VMEM((1, LANES), F32)],
        compiler_params=_params("arbitrary", "arbitrary"),
        name="in_proj",
    )(x, mod3, g1, wmain, wsmall, wa2, ba, bfox, cos_t, sin_t, qg, kg, tri)


def _mixer_kernel(r_ref, gl_ref, la_ref, mret_ref, qw_ref, kw_ref, gts_ref, btri_ref, bones_ref,
                  rg_ref, gg_ref, ret_ref, gla_ref, rstate_ref, gstate_ref):
    w = BRANCH_W
    ts = r_ref.shape[1]

    @pl.when(pl.program_id(1) == 0)
    def _():
        rstate_ref[...] = jnp.zeros_like(rstate_ref)
        gstate_ref[...] = jnp.zeros_like(gstate_ref)

    for h in range(N_HEADS):
        sl = slice(h * HEAD_DIM, (h + 1) * HEAD_DIM)
        q = r_ref[0, :, sl]
        k = r_ref[0, :, w + h * HEAD_DIM:w + (h + 1) * HEAD_DIM]
        v = r_ref[0, :, 2 * w + h * HEAD_DIM:2 * w + (h + 1) * HEAD_DIM]
        gate = r_ref[0, :, 3 * w + h * HEAD_DIM:3 * w + (h + 1) * HEAD_DIM].astype(F32)
        state = rstate_ref[h]
        p = (_dot_nt(q, k) * mret_ref[h]).astype(BF16)
        o = _dot(p, v) + _dot(q, state.astype(BF16)) * qw_ref[h]
        kk = (k.astype(F32) * kw_ref[h]).astype(BF16)
        rstate_ref[h] = gts_ref[h] * state + _dot_tn(kk, v)
        mu = jnp.mean(o, axis=-1, keepdims=True)
        oc = o - mu
        var = jnp.mean(oc * oc, axis=-1, keepdims=True)
        on = oc * lax.rsqrt(var + EPS) * rg_ref[:, sl]
        ret_ref[0, :, sl] = (_silu(gate) * on).astype(BF16)

    la = la_ref[0]
    b_cum = _split3_dot(btri_ref[...], la)
    b_end = _split3_dot(bones_ref[...], la)
    kdec = jnp.exp(b_end - b_cum)
    a_all = jnp.exp(b_end)
    nk = N_HEADS * GLA_DK
    kk_all = (gl_ref[0, :, nk:2 * nk].astype(F32) * kdec).astype(BF16)
    q_all = gl_ref[0, :, 0:nk]
    lane = lax.broadcasted_iota(jnp.int32, (ts, LANES), 1)
    for h in range(N_HEADS):
        pair = h // 2
        psl = slice(pair * LANES, (pair + 1) * LANES)
        own = (lane >= (h % 2) * GLA_DK) & (lane < (h % 2 + 1) * GLA_DK)
        qm = jnp.where(own, q_all[:, psl], jnp.zeros((), BF16))
        kk = kk_all[:, psl]
        v = gl_ref[0, :, w + h * HEAD_DIM:w + (h + 1) * HEAD_DIM]
        gate = gl_ref[0, :, 2 * w + h * HEAD_DIM:2 * w + (h + 1) * HEAD_DIM].astype(F32)
        state = gstate_ref[h]
        outs = []
        for c in range(ts // CHUNK):
            rows = slice(c * CHUNK, (c + 1) * CHUNK)
            a_row = a_all[c * CHUNK:c * CHUNK + 1, psl]
            state = a_row * state + _dot_tn(v[rows], kk[rows])
            outs.append(_dot_nt(qm[rows], state.astype(BF16)))
        gstate_ref[h] = state
        o = jnp.concatenate(outs, axis=0)
        ms = jnp.mean(o * o, axis=-1, keepdims=True)
        on = o * lax.rsqrt(ms + EPS) * gg_ref[...]
        gla_ref[0, :, h * HEAD_DIM:(h + 1) * HEAD_DIM] = (_silu(gate) * on).astype(BF16)


def _mixer_call(r, gl, la, mret, qw, kw, gts, btri, bones, rg, gg):
    b, s, _ = r.shape
    ts = mret.shape[1]
    w = BRANCH_W
    tile = lambda i, j: (i, j, 0)
    c2 = lambda i, j: (0, 0)
    c3 = lambda i, j: (0, 0, 0)
    return pl.pallas_call(
        _mixer_kernel,
        out_shape=(jax.ShapeDtypeStruct((b, s, w), BF16), jax.ShapeDtypeStruct((b, s, w), BF16)),
        grid=(b, s // ts),
        in_specs=[
            pl.BlockSpec((1, ts, 4 * w), tile),
            pl.BlockSpec((1, ts, 3 * w), tile),
            pl.BlockSpec((1, ts, N_HEADS * GLA_DK), tile),
            pl.BlockSpec(mret.shape, c3),
            pl.BlockSpec(qw.shape, c3),
            pl.BlockSpec(kw.shape, c3),
            pl.BlockSpec(gts.shape, c3),
            pl.BlockSpec(btri.shape, c2),
            pl.BlockSpec(bones.shape, c2),
            pl.BlockSpec(rg.shape, c2),
            pl.BlockSpec(gg.shape, c2),
        ],
        out_specs=(pl.BlockSpec((1, ts, w), tile), pl.BlockSpec((1, ts, w), tile)),
        scratch_shapes=[pltpu.VMEM((N_HEADS, HEAD_DIM, HEAD_DIM), F32),
                        pltpu.VMEM((N_HEADS, HEAD_DIM, LANES), F32)],
        compiler_params=_params("arbitrary", "arbitrary"),
        name="ret_gla_mixer",
    )(r, gl, la, mret, qw, kw, gts, btri, bones, rg, gg)


def _fox_kernel(q_ref, k_ref, v_ref, ccol_ref, crow_ref, o_ref):
    tq = q_ref.shape[1]
    qi = pl.program_id(2)
    q = q_ref[0]
    ct = ccol_ref[0, 0]

    def scores(j):
        k = k_ref[0, pl.ds(pl.multiple_of(j * tq, tq), tq), :]
        return _dot_nt(q, k) + (ct - crow_ref[0, 0, pl.ds(j, 1), :])

    def update(j, s, carry):
        m, l, acc = carry
        m_new = jnp.maximum(m, jnp.max(s, axis=-1, keepdims=True))
        alpha = jnp.exp(m - m_new)
        p = jnp.exp(s - m_new)
        v = v_ref[0, pl.ds(pl.multiple_of(j * tq, tq), tq), :]
        l = alpha * l + jnp.sum(p, axis=-1, keepdims=True)
        acc = alpha * acc + _dot(p.astype(BF16), v)
        return m_new, l, acc

    def body(j, carry):
        return update(j, scores(j), carry)

    init = (jnp.full((tq, 1), -jnp.inf, F32), jnp.zeros((tq, 1), F32), jnp.zeros((tq, HEAD_DIM), F32))
    carry = lax.fori_loop(0, qi, body, init)
    row = lax.broadcasted_iota(jnp.int32, (tq, tq), 0)
    col = lax.broadcasted_iota(jnp.int32, (tq, tq), 1)
    s = jnp.where(row >= col, scores(qi), -jnp.inf)
    m, l, acc = update(qi, s, carry)
    o_ref[0] = (acc / l).astype(BF16)


def _fox_call(f, ccol, crow):
    b, s, _ = f.shape
    tq = crow.shape[3]
    return pl.pallas_call(
        _fox_kernel,
        out_shape=jax.ShapeDtypeStruct((b, s, BRANCH_W), BF16),
        grid=(b, N_HEADS, s // tq),
        in_specs=[
            pl.BlockSpec((1, tq, HEAD_DIM), lambda i, h, j: (i, j, h)),
            pl.BlockSpec((1, s, HEAD_DIM), lambda i, h, j: (i, 0, N_HEADS + h)),
            pl.BlockSpec((1, s, HEAD_DIM), lambda i, h, j: (i, 0, 2 * N_HEADS + h)),
            pl.BlockSpec((1, 1, tq, 1), lambda i, h, j: (i, h, j, 0)),
            pl.BlockSpec((1, 1, s // tq, tq), lambda i, h, j: (i, h, 0, 0)),
        ],
        out_specs=pl.BlockSpec((1, tq, HEAD_DIM), lambda i, h, j: (i, j, h)),
        compiler_params=_params("arbitrary", "arbitrary", "arbitrary"),
        name="fox_attention",
    )(f, f, f, ccol, crow)


def _outproj_kernel(x_ref, mod_ref, g1_ref, ret_ref, gla_ref, fox_ref, wbr_ref, wmg_ref, bmg_ref, wo_ref,
                    o_ref):
    d = x_ref.shape[2]
    x = x_ref[0]
    mod = mod_ref[0]
    hb = _norm_modulate(x, g1_ref[...], mod[:, 0:d], mod[:, d:2 * d]).astype(BF16)
    mixed = None
    for n, br_ref in enumerate((ret_ref, gla_ref, fox_ref)):
        y = _dot(br_ref[0], wbr_ref[n])
        gate = _sigmoid(_dot(hb, wmg_ref[:, n * d:(n + 1) * d]) + bmg_ref[:, n * d:(n + 1) * d])
        mixed = gate * y if mixed is None else mixed + gate * y
    out = _dot(mixed.astype(BF16), wo_ref[...])
    o_ref[0] = x + mod[:, 2 * d:3 * d] * out


def _outproj_call(x, mod3, g1, ret, gla, fox, wbr, wmg, bmg, wo):
    b, s, d = x.shape
    ts = min(TS_OUT, s)
    tile = lambda i, j: (i, j, 0)
    c2 = lambda i, j: (0, 0)
    return pl.pallas_call(
        _outproj_kernel,
        out_shape=jax.ShapeDtypeStruct((b, s, d), F32),
        grid=(b, s // ts),
        in_specs=[
            pl.BlockSpec((1, ts, d), tile),
            pl.BlockSpec((1, 1, mod3.shape[2]), lambda i, j: (i, 0, 0)),
            pl.BlockSpec(g1.shape, c2),
            pl.BlockSpec((1, ts, BRANCH_W), tile),
            pl.BlockSpec((1, ts, BRANCH_W), tile),
            pl.BlockSpec((1, ts, BRANCH_W), tile),
            pl.BlockSpec(wbr.shape, lambda i, j: (0, 0, 0)),
            pl.BlockSpec(wmg.shape, c2),
            pl.BlockSpec(bmg.shape, c2),
            pl.BlockSpec(wo.shape, c2),
        ],
        out_specs=pl.BlockSpec((1, ts, d), tile),
        compiler_params=_params("arbitrary", "arbitrary"),
        name="out_proj",
    )(x, mod3, g1, ret, gla, fox, wbr, wmg, bmg, wo)


def _ffn_kernel(x_ref, mod_ref, g2_ref, wup_ref, wconv_ref, bconv_ref, wdown_ref, o_ref, tail_ref):
    d = x_ref.shape[2]
    ts = x_ref.shape[1]
    ff = wdown_ref.shape[0]

    @pl.when(pl.program_id(1) == 0)
    def _():
        tail_ref[...] = jnp.zeros_like(tail_ref)

    x = x_ref[0]
    mod = mod_ref[0]
    hb = _norm_modulate(x, g2_ref[...], mod[:, 3 * d:4 * d], mod[:, 4 * d:5 * d]).astype(BF16)
    u = _dot(hb, wup_ref[:, 0:ff])
    g = _dot(hb, wup_ref[:, ff:2 * ff])
    row = lax.broadcasted_iota(jnp.int32, u.shape, 0)
    prev1 = tail_ref[CONV_W - 2:CONV_W - 1, :]
    prev2 = tail_ref[CONV_W - 3:CONV_W - 2, :]
    u1 = jnp.where(row == 0, prev1, pltpu.roll(u, 1, axis=0))
    u2 = jnp.where(row == 0, prev2, jnp.where(row == 1, prev1, pltpu.roll(u, 2, axis=0)))
    tail_ref[...] = u[ts - (CONV_W - 1):ts, :]
    wc = wconv_ref[...]
    conv = bconv_ref[...] + wc[0:1, :] * u2 + wc[1:2, :] * u1 + wc[2:3, :] * u
    act = (_silu(conv) * g).astype(BF16)
    y = _dot(act, wdown_ref[...])
    o_ref[0] = x + mod[:, 5 * d:6 * d] * y


def _ffn_call(x, mod3, g2, wup, wconv, bconv, wdown):
    b, s, d = x.shape
    ts = min(TS_FFN, s)
    ff = wdown.shape[0]
    tile = lambda i, j: (i, j, 0)
    c2 = lambda i, j: (0, 0)
    return pl.pallas_call(
        _ffn_kernel,
        out_shape=jax.ShapeDtypeStruct((b, s, d), F32),
        grid=(b, s // ts),
        in_specs=[
            pl.BlockSpec((1, ts, d), tile),
            pl.BlockSpec((1, 1, mod3.shape[2]), lambda i, j: (i, 0, 0)),
            pl.BlockSpec(g2.shape, c2),
            pl.BlockSpec(wup.shape, c2),
            pl.BlockSpec(wconv.shape, c2),
            pl.BlockSpec(bconv.shape, c2),
            pl.BlockSpec(wdown.shape, c2),
        ],
        out_specs=pl.BlockSpec((1, ts, d), tile),
        scratch_shapes=[pltpu.VMEM((CONV_W - 1, ff), F32)],
        compiler_params=_params("arbitrary", "arbitrary"),
        name="conv_ffn",
    )(x, mod3, g2, wup, wconv, bconv, wdown)


def _rope_tables(s):
    half = HEAD_DIM // 2
    pos = jnp.arange(s, dtype=F32)
    inv_freq = ROPE_BASE ** (-jnp.arange(half, dtype=F32) / half)
    ang = pos[:, None] * inv_freq[None, :]
    cos, sin = jnp.cos(ang), jnp.sin(ang)
    return jnp.concatenate([cos, cos], axis=1), jnp.concatenate([-sin, sin], axis=1)


def _retention_tables(ts):
    log_g = jnp.log1p(-jnp.exp2(-5.0 - jnp.arange(N_HEADS, dtype=F32)))[:, None, None]
    t = jnp.arange(ts, dtype=F32)
    diff = t[:, None] - t[None, :]
    same = (jnp.arange(ts)[:, None] // CHUNK) == (jnp.arange(ts)[None, :] // CHUNK)
    expo = jnp.where(same, jnp.abs(diff), diff)[None]
    mask = jnp.where((same | (diff > 0))[None], jnp.exp(jnp.where(expo >= 0, expo, 0.0) * log_g), 0.0)
    ones = jnp.ones((1, 1, HEAD_DIM), F32)
    qw = jnp.exp((t + 1.0)[None, :, None] * log_g) * ones
    kw = jnp.exp((ts - 1.0 - t)[None, :, None] * log_g) * ones
    gts = jnp.exp(ts * log_g) * ones
    return mask, qw, kw, gts


def _chunk_matrices(ts):
    i = jnp.arange(ts)
    same = (i[:, None] // CHUNK) == (i[None, :] // CHUNK)
    btri = (same & (i[:, None] >= i[None, :])).astype(BF16)
    return btri, same.astype(BF16)


def _layer(x, mod3, tables, norm1_g, norm2_g, w_in, w_gla_a2, b_gla_a, b_fox_f, ret_norm_g, gla_norm_g,
           q_norm_g, k_norm_g, w_br, w_mg, b_mg, w_o, w_up, w_conv, b_conv, w_down):
    b, s, d = x.shape
    w = BRANCH_W
    cos_t, sin_t, tri, mret, qw, kw, gts, btri, bones = tables
    nk = N_HEADS * GLA_DK
    o_glr = 4 * w + 2 * nk + w
    o_gg = o_glr + GLA_LOWRANK
    o_fq = o_gg + w
    o_ff = o_fq + 3 * w
    wmain = jnp.concatenate(
        [w_in[:, :o_glr], w_in[:, o_gg:o_ff]], axis=1).astype(BF16)
    wsmall = jnp.concatenate(
        [w_in[:, o_glr:o_gg], w_in[:, o_ff:o_ff + N_HEADS],
         jnp.zeros((d, LANES - GLA_LOWRANK - N_HEADS), F32)], axis=1).astype(BF16)
    wa2 = jnp.concatenate([w_gla_a2, jnp.zeros((LANES - GLA_LOWRANK, nk), F32)], axis=0).astype(BF16)
    bfox = jnp.zeros((1, LANES), F32).at[0, FF_LANE0:FF_LANE0 + N_HEADS].set(b_fox_f)

    r, gl, f, la, cum = _inproj_call(
        x, mod3, norm1_g[None], wmain, wsmall, wa2, b_gla_a[None], bfox, cos_t, sin_t,
        q_norm_g[None], k_norm_g[None], tri)

    ret, gla_o = _mixer_call(r, gl, la, mret, qw, kw, gts, btri, bones, ret_norm_g[None], gla_norm_g[None])

    tq = min(TQ_FOX, s)
    cum_h = jnp.transpose(cum[:, :, FF_LANE0:FF_LANE0 + N_HEADS], (0, 2, 1))
    fox = _fox_call(f, cum_h[..., None], cum_h.reshape(b, N_HEADS, s // tq, tq))

    x = _outproj_call(x, mod3, norm1_g[None], ret, gla_o, fox, w_br.astype(BF16), w_mg.astype(BF16),
                      b_mg[None], w_o.astype(BF16))
    x = _ffn_call(x, mod3, norm2_g[None], w_up.astype(BF16), w_conv, b_conv[None], w_down.astype(BF16))
    return x


def kernel(x, c, norm1_g, norm2_g, w_ada, b_ada, w_in, w_gla_a2, b_gla_a, b_fox_f, ret_norm_g, gla_norm_g,
           q_norm_g, k_norm_g, w_br, w_mg, b_mg, w_o, w_up, w_conv, b_conv, w_down):
    b, s, d = x.shape
    depth = w_ada.shape[0]
    mod = _ada_call(c, w_ada, b_ada)
    ts_in = min(TS_IN, s)
    ts_mix = min(TS_MIX, s)
    i = jnp.arange(ts_in)
    tri = (i[:, None] >= i[None, :]).astype(BF16)
    tables = _rope_tables(s) + (tri,) + _retention_tables(ts_mix) + _chunk_matrices(ts_mix)
    for l in range(depth):
        x = _layer(x, mod[l][:, None, :], tables, norm1_g[l], norm2_g[l], w_in[l], w_gla_a2[l], b_gla_a[l],
                   b_fox_f[l], ret_norm_g[l], gla_norm_g[l], q_norm_g[l], k_norm_g[l], w_br[l], w_mg[l],
                   b_mg[l], w_o[l], w_up[l], w_conv[l], b_conv[l], w_down[l])
    return x
```

```python
import jax
import jax.numpy as jnp
from jax import lax
from jax.experimental import pallas as pl
from jax.experimental.pallas import tpu as pltpu

F32 = jnp.float32
BF16 = jnp.bfloat16

CHUNK = 64
N_HEADS = 4
HEAD_DIM = 128
BRANCH_W = N_HEADS * HEAD_DIM
GLA_DK = HEAD_DIM // 2
GLA_LOWRANK = 16
GLA_TAU = 16.0
N_BRANCH = 3
CONV_W = 3
ROPE_BASE = 10000.0
EPS = 1e-6

LANES = 128
SUBLANES = 8
FF_LANE0 = GLA_LOWRANK

VMEM_LIMIT = 56 * 1024 * 1024

TS_IN = 256
TS_MIX = 256
TS_OUT = 256
TS_FFN = 256
TR_PREP = 256

_NK = N_HEADS * GLA_DK
O_GLR = 4 * BRANCH_W + 2 * _NK + BRANCH_W
O_GG = O_GLR + GLA_LOWRANK
O_FF = O_GG + 4 * BRANCH_W
W_MAIN = O_GLR + (O_FF - O_GG)


def _dot(a, b):
    return jnp.dot(a, b, preferred_element_type=F32)


def _dot_nt(a, b):
    return lax.dot_general(a, b, (((1,), (1,)), ((), ())), preferred_element_type=F32)


def _dot_tn(a, b):
    return lax.dot_general(a, b, (((0,), (0,)), ((), ())), preferred_element_type=F32)


def _split3(x):
    x_hi = x.astype(BF16)
    r1 = x - x_hi.astype(F32)
    x_mid = r1.astype(BF16)
    x_lo = (r1 - x_mid.astype(F32)).astype(BF16)
    return x_hi, x_mid, x_lo


def _split3_dot(m_bf16, x):
    x_hi, x_mid, x_lo = _split3(x)
    return _dot(m_bf16, x_hi) + _dot(m_bf16, x_mid) + _dot(m_bf16, x_lo)


def _log_sigmoid(x):
    return jnp.minimum(x, 0.0) - jnp.log1p(jnp.exp(-jnp.abs(x)))


def _sigmoid(x):
    return 1.0 / (1.0 + jnp.exp(-x))


def _silu(x):
    return x * _sigmoid(x)


def _norm_modulate(x, g, shift, scale):
    ms = jnp.mean(x * x, axis=-1, keepdims=True)
    y = x * lax.rsqrt(ms + EPS) * g
    return y * (1.0 + scale) + shift


def _params(*sem):
    return pltpu.CompilerParams(dimension_semantics=sem, vmem_limit_bytes=VMEM_LIMIT)


def _layer_spec(arr, l):
    zeros = (0,) * (arr.ndim - 1)
    return pl.BlockSpec((1,) + arr.shape[1:], lambda *_: (l,) + zeros)


def _const_spec(arr):
    zeros = (0,) * arr.ndim
    return pl.BlockSpec(arr.shape, lambda *_: zeros)


def _ada_kernel(c_ref, w_ref, b_ref, o_ref):
    c_act = _silu(c_ref[...]).astype(BF16)
    o_ref[0] = _dot(c_act, w_ref[0].astype(BF16)) + b_ref[0]


def _ada_call(c, w_ada, b_ada):
    depth, d, n = w_ada.shape
    b = c.shape[0]
    tn = n // 4
    return pl.pallas_call(
        _ada_kernel,
        out_shape=jax.ShapeDtypeStruct((depth, b, n), F32),
        grid=(depth, n // tn),
        in_specs=[
            pl.BlockSpec((b, d), lambda l, j: (0, 0)),
            pl.BlockSpec((1, d, tn), lambda l, j: (l, 0, j)),
            pl.BlockSpec((1, 1, tn), lambda l, j: (l, 0, j)),
        ],
        out_specs=pl.BlockSpec((1, b, tn), lambda l, j: (l, 0, j)),
        compiler_params=_params("arbitrary", "arbitrary"),
        name="ada_mod",
    )(c, w_ada, b_ada.reshape(depth, 1, n))


def _win_prep_kernel(w_ref, main_ref, small_ref):
    tr = w_ref.shape[1]
    main_ref[0, :, 0:O_GLR] = w_ref[0, :, 0:O_GLR].astype(BF16)
    main_ref[0, :, O_GLR:W_MAIN] = w_ref[0, :, O_GG:O_FF].astype(BF16)
    small = jnp.concatenate(
        [w_ref[0, :, O_GLR:O_GG], w_ref[0, :, O_FF:O_FF + N_HEADS],
         jnp.zeros((tr, LANES - GLA_LOWRANK - N_HEADS), F32)], axis=1)
    small_ref[0] = small.astype(BF16)


def _win_prep_call(w_in):
    depth, d, n = w_in.shape
    tr = TR_PREP
    tile = lambda l, i: (l, i, 0)
    return pl.pallas_call(
        _win_prep_kernel,
        out_shape=(jax.ShapeDtypeStruct((depth, d, W_MAIN), BF16),
                   jax.ShapeDtypeStruct((depth, d, LANES), BF16)),
        grid=(depth, d // tr),
        in_specs=[pl.BlockSpec((1, tr, n), tile)],
        out_specs=(pl.BlockSpec((1, tr, W_MAIN), tile), pl.BlockSpec((1, tr, LANES), tile)),
        compiler_params=_params("arbitrary", "arbitrary"),
        name="win_prep",
    )(w_in)


def _inproj_kernel(x_ref, mod_ref, g1_ref, wmain_ref, wsmall_ref, wa2_ref, ba_ref, bfox_ref,
                   cos_ref, sin_ref, qg_ref, kg_ref, tri_ref, sel_ref,
                   r_ref, gl_ref, f_ref, la_ref, cum_ref, crow_ref, carry_ref):
    d = x_ref.shape[2]
    w = BRANCH_W

    @pl.when(pl.program_id(1) == 0)
    def _():
        carry_ref[...] = jnp.zeros_like(carry_ref)

    x = x_ref[0]
    mod = mod_ref[0, 0]
    h = _norm_modulate(x, g1_ref[0], mod[:, 0:d], mod[:, d:2 * d])
    hb = h.astype(BF16)
    cos = cos_ref[...]
    sin = sin_ref[...]

    def heads_map(acc, fn):
        return jnp.concatenate(
            [fn(acc[:, i * HEAD_DIM:(i + 1) * HEAD_DIM]) for i in range(N_HEADS)], axis=1)

    def rope(t):
        return t * cos + pltpu.roll(t, HEAD_DIM // 2, axis=1) * sin

    def qk_norm(g):
        def fn(t):
            ms = jnp.mean(t * t, axis=-1, keepdims=True)
            return t * lax.rsqrt(ms + EPS) * g
        return fn

    def proj(c0, n):
        return _dot(hb, wmain_ref[0, :, c0:c0 + n])

    r_ref[0, :, 0:w] = heads_map(proj(0, w), rope).astype(BF16)
    r_ref[0, :, w:2 * w] = (heads_map(proj(w, w), rope) * HEAD_DIM ** -0.5).astype(BF16)
    r_ref[0, :, 2 * w:4 * w] = proj(2 * w, 2 * w).astype(BF16)
    c0 = 4 * w
    gqk = proj(c0, w)
    lane = lax.broadcasted_iota(jnp.int32, gqk.shape, 1)
    gqk = jnp.where(lane < _NK, gqk * GLA_DK ** -0.5, gqk)
    gl_ref[0, :, 0:w] = gqk.astype(BF16)
    gl_ref[0, :, w:3 * w] = proj(c0 + w, 2 * w).astype(BF16)
    c0 = 7 * w
    f_ref[0, :, 0:w] = (heads_map(proj(c0, w), qk_norm(qg_ref[0])) * HEAD_DIM ** -0.5).astype(BF16)
    f_ref[0, :, w:2 * w] = heads_map(proj(c0 + w, w), qk_norm(kg_ref[0])).astype(BF16)
    f_ref[0, :, 2 * w:3 * w] = proj(c0 + 2 * w, w).astype(BF16)

    small = _dot(hb, wsmall_ref[0])
    la_pre = _dot(small.astype(BF16), wa2_ref[0]) + ba_ref[0]
    la_ref[0] = _log_sigmoid(la_pre) * (1.0 / GLA_TAU)
    lane = lax.broadcasted_iota(jnp.int32, small.shape, 1)
    is_ff = (lane >= FF_LANE0) & (lane < FF_LANE0 + N_HEADS)
    log_f = jnp.where(is_ff, _log_sigmoid(small + bfox_ref[0]), 0.0)
    cum = _split3_dot(tri_ref[...], log_f) + carry_ref[...]
    cum_ref[0] = cum
    carry_ref[...] = cum[cum.shape[0] - 1:cum.shape[0], :]
    sel = sel_ref[...]
    c_hi, c_mid, c_lo = _split3(cum)
    crow_ref[0, 0] = _dot_nt(sel, c_hi) + _dot_nt(sel, c_mid) + _dot_nt(sel, c_lo)


def _inproj_call(l, x, mod4, g1, wmain, wsmall, wa2, ba, bfox, cos_t, sin_t, qg, kg, tri, sel):
    b, s, d = x.shape
    ts = min(TS_IN, s)
    w = BRANCH_W
    tile = lambda i, j: (i, j, 0)
    return pl.pallas_call(
        _inproj_kernel,
        out_shape=(
            jax.ShapeDtypeStruct((b, s, 4 * w), BF16),
            jax.ShapeDtypeStruct((b, s, 3 * w), BF16),
            jax.ShapeDtypeStruct((b, s, 3 * w), BF16),
            jax.ShapeDtypeStruct((b, s, _NK), F32),
            jax.ShapeDtypeStruct((b, s, LANES), F32),
            jax.ShapeDtypeStruct((b, s // ts, SUBLANES, ts), F32),
        ),
        grid=(b, s // ts),
        in_specs=[
            pl.BlockSpec((1, ts, d), tile),
            pl.BlockSpec((1, 1, 1, mod4.shape[3]), lambda i, j: (l, i, 0, 0)),
            _layer_spec(g1, l),
            _layer_spec(wmain, l),
            _layer_spec(wsmall, l),
            _layer_spec(wa2, l),
            _layer_spec(ba, l),
            _layer_spec(bfox, l),
            pl.BlockSpec((ts, HEAD_DIM), lambda i, j: (j, 0)),
            pl.BlockSpec((ts, HEAD_DIM), lambda i, j: (j, 0)),
            _layer_spec(qg, l),
            _layer_spec(kg, l),
            _const_spec(tri),
            _const_spec(sel),
        ],
        out_specs=(
            pl.BlockSpec((1, ts, 4 * w), tile),
            pl.BlockSpec((1, ts, 3 * w), tile),
            pl.BlockSpec((1, ts, 3 * w), tile),
            pl.BlockSpec((1, ts, _NK), tile),
            pl.BlockSpec((1, ts, LANES), tile),
            pl.BlockSpec((1, 1, SUBLANES, ts), lambda i, j: (i, j, 0, 0)),
        ),
        scratch_shapes=[pltpu.VMEM((1, LANES), F32)],
        compiler_params=_params("arbitrary", "arbitrary"),
        name="in_proj",
    )(x, mod4, g1, wmain, wsmall, wa2, ba, bfox, cos_t, sin_t, qg, kg, tri, sel)


def _mixer_kernel(r_ref, gl_ref, la_ref, mret_ref, qw_ref, kw_ref, gts_ref, btri_ref, bones_ref,
                  rg_ref, gg_ref, ret_ref, gla_ref, rstate_ref, gstate_ref):
    w = BRANCH_W
    ts = r_ref.shape[1]

    @pl.when(pl.program_id(1) == 0)
    def _():
        rstate_ref[...] = jnp.zeros_like(rstate_ref)
        gstate_ref[...] = jnp.zeros_like(gstate_ref)

    for h in range(N_HEADS):
        sl = slice(h * HEAD_DIM, (h + 1) * HEAD_DIM)
        q = r_ref[0, :, sl]
        k = r_ref[0, :, w + h * HEAD_DIM:w + (h + 1) * HEAD_DIM]
        v = r_ref[0, :, 2 * w + h * HEAD_DIM:2 * w + (h + 1) * HEAD_DIM]
        gate = r_ref[0, :, 3 * w + h * HEAD_DIM:3 * w + (h + 1) * HEAD_DIM].astype(F32)
        state = rstate_ref[h]
        p = (_dot_nt(q, k) * mret_ref[h]).astype(BF16)
        o = _dot(p, v) + _dot(q, state.astype(BF16)) * qw_ref[h]
        kk = (k.astype(F32) * kw_ref[h]).astype(BF16)
        rstate_ref[h] = gts_ref[h] * state + _dot_tn(kk, v)
        mu = jnp.mean(o, axis=-1, keepdims=True)
        oc = o - mu
        var = jnp.mean(oc * oc, axis=-1, keepdims=True)
        on = oc * lax.rsqrt(var + EPS) * rg_ref[0, :, sl]
        ret_ref[0, :, sl] = (_silu(gate) * on).astype(BF16)

    la = la_ref[0]
    b_cum = _split3_dot(btri_ref[...], la)
    b_end = _split3_dot(bones_ref[...], la)
    kdec = jnp.exp(b_end - b_cum)
    a_all = jnp.exp(b_end)
    kk_all = (gl_ref[0, :, _NK:2 * _NK].astype(F32) * kdec).astype(BF16)
    q_all = gl_ref[0, :, 0:_NK]
    lane = lax.broadcasted_iota(jnp.int32, (ts, LANES), 1)
    for h in range(N_HEADS):
        pair = h // 2
        psl = slice(pair * LANES, (pair + 1) * LANES)
        own = (lane >= (h % 2) * GLA_DK) & (lane < (h % 2 + 1) * GLA_DK)
        qm = jnp.where(own, q_all[:, psl], jnp.zeros((), BF16))
        kk = kk_all[:, psl]
        v = gl_ref[0, :, w + h * HEAD_DIM:w + (h + 1) * HEAD_DIM]
        gate = gl_ref[0, :, 2 * w + h * HEAD_DIM:2 * w + (h + 1) * HEAD_DIM].astype(F32)
        state = gstate_ref[h]
        outs = []
        for c in range(ts // CHUNK):
            rows = slice(c * CHUNK, (c + 1) * CHUNK)
            a_row = a_all[c * CHUNK:c * CHUNK + 1, psl]
            state = a_row * state + _dot_tn(v[rows], kk[rows])
            outs.append(_dot_nt(qm[rows], state.astype(BF16)))
        gstate_ref[h] = state
        o = jnp.concatenate(outs, axis=0)
        ms = jnp.mean(o * o, axis=-1, keepdims=True)
        on = o * lax.rsqrt(ms + EPS) * gg_ref[0]
        gla_ref[0, :, h * HEAD_DIM:(h + 1) * HEAD_DIM] = (_silu(gate) * on).astype(BF16)


def _mixer_call(l, r, gl, la, mret, qw, kw, gts, btri, bones, rg, gg):
    b, s, _ = r.shape
    ts = mret.shape[1]
    w = BRANCH_W
    tile = lambda i, j: (i, j, 0)
    return pl.pallas_call(
        _mixer_kernel,
        out_shape=(jax.ShapeDtypeStruct((b, s, w), BF16), jax.ShapeDtypeStruct((b, s, w), BF16)),
        grid=(b, s // ts),
        in_specs=[
            pl.BlockSpec((1, ts, 4 * w), tile),
            pl.BlockSpec((1, ts, 3 * w), tile),
            pl.BlockSpec((1, ts, _NK), tile),
            _const_spec(mret),
            _const_spec(qw),
            _const_spec(kw),
            _const_spec(gts),
            _const_spec(btri),
            _const_spec(bones),
            _layer_spec(rg, l),
            _layer_spec(gg, l),
        ],
        out_specs=(pl.BlockSpec((1, ts, w), tile), pl.BlockSpec((1, ts, w), tile)),
        scratch_shapes=[pltpu.VMEM((N_HEADS, HEAD_DIM, HEAD_DIM), F32),
                        pltpu.VMEM((N_HEADS, HEAD_DIM, LANES), F32)],
        compiler_params=_params("arbitrary", "arbitrary"),
        name="ret_gla_mixer",
    )(r, gl, la, mret, qw, kw, gts, btri, bones, rg, gg)


def _fox_kernel(q_ref, k_ref, v_ref, cum_ref, crow_ref, o_ref, m_ref, l_ref, acc_ref):
    tq = q_ref.shape[1]
    tk = 2 * tq
    qi = pl.program_id(1)
    hs = [slice(h * HEAD_DIM, (h + 1) * HEAD_DIM) for h in range(N_HEADS)]
    m_ref[...] = jnp.full(m_ref.shape, -jnp.inf, F32)
    l_ref[...] = jnp.zeros_like(l_ref)
    acc_ref[...] = jnp.zeros_like(acc_ref)

    def block(jb, mask):
        rows = pl.ds(pl.multiple_of(jb * tk, tk), tk)
        crow = jnp.concatenate([crow_ref[0, 2 * jb], crow_ref[0, 2 * jb + 1]], axis=1)
        for h in range(N_HEADS):
            ct = cum_ref[0, :, FF_LANE0 + h:FF_LANE0 + h + 1]
            s = _dot_nt(q_ref[0, :, hs[h]], k_ref[0, rows, hs[h]]) + (ct - crow[h:h + 1, :])
            if mask is not None:
                s = jnp.where(mask, s, -jnp.inf)
            m = m_ref[h]
            m_new = jnp.maximum(m, jnp.max(s, axis=-1, keepdims=True))
            alpha = jnp.exp(m - m_new)
            p = jnp.exp(s - m_new)
            l_ref[h] = alpha * l_ref[h] + jnp.sum(p, axis=-1, keepdims=True)
            acc_ref[h] = alpha * acc_ref[h] + _dot(p.astype(BF16), v_ref[0, rows, hs[h]])
            m_ref[h] = m_new

    def body(jb, carry):
        block(jb, None)
        return carry

    n_full = qi // 2
    lax.fori_loop(0, n_full, body, 0)
    row = lax.broadcasted_iota(jnp.int32, (tq, tk), 0)
    col = lax.broadcasted_iota(jnp.int32, (tq, tk), 1)
    block(n_full, col <= row + tq * (qi % 2))
    for h in range(N_HEADS):
        o_ref[0, :, hs[h]] = (acc_ref[h] / l_ref[h]).astype(BF16)


def _fox_call(f, cum, crow):
    b, s, _ = f.shape
    nq, tq = crow.shape[1], crow.shape[3]
    w = BRANCH_W
    assert s % (2 * tq) == 0
    return pl.pallas_call(
        _fox_kernel,
        out_shape=jax.ShapeDtypeStruct((b, s, w), BF16),
        grid=(b, nq),
        in_specs=[
            pl.BlockSpec((1, tq, w), lambda i, j: (i, j, 0)),
            pl.BlockSpec((1, s, w), lambda i, j: (i, 0, 1)),
            pl.BlockSpec((1, s, w), lambda i, j: (i, 0, 2)),
            pl.BlockSpec((1, tq, LANES), lambda i, j: (i, j, 0)),
            pl.BlockSpec((1, nq, SUBLANES, tq), lambda i, j: (i, 0, 0, 0)),
        ],
        out_specs=pl.BlockSpec((1, tq, w), lambda i, j: (i, j, 0)),
        scratch_shapes=[pltpu.VMEM((N_HEADS, tq, 1), F32), pltpu.VMEM((N_HEADS, tq, 1), F32),
                        pltpu.VMEM((N_HEADS, tq, HEAD_DIM), F32)],
        compiler_params=_params("arbitrary", "arbitrary"),
        name="fox_attention",
    )(f, f, f, cum, crow)


def _outproj_kernel(x_ref, mod_ref, g1_ref, ret_ref, gla_ref, fox_ref, wbr_ref, wmg_ref, bmg_ref, wo_ref,
                    o_ref):
    d = x_ref.shape[2]
    x = x_ref[0]
    mod = mod_ref[0, 0]
    hb = _norm_modulate(x, g1_ref[0], mod[:, 0:d], mod[:, d:2 * d]).astype(BF16)
    mixed = None
    for n, br_ref in enumerate((ret_ref, gla_ref, fox_ref)):
        y = _dot(br_ref[0], wbr_ref[0, n])
        gate = _sigmoid(_dot(hb, wmg_ref[0, :, n * d:(n + 1) * d]) + bmg_ref[0, :, n * d:(n + 1) * d])
        mixed = gate * y if mixed is None else mixed + gate * y
    out = _dot(mixed.astype(BF16), wo_ref[0])
    o_ref[0] = x + mod[:, 2 * d:3 * d] * out


def _outproj_call(l, x, mod4, g1, ret, gla, fox, wbr, wmg, bmg, wo):
    b, s, d = x.shape
    ts = min(TS_OUT, s)
    tile = lambda i, j: (i, j, 0)
    return pl.pallas_call(
        _outproj_kernel,
        out_shape=jax.ShapeDtypeStruct((b, s, d), F32),
        grid=(b, s // ts),
        in_specs=[
            pl.BlockSpec((1, ts, d), tile),
            pl.BlockSpec((1, 1, 1, mod4.shape[3]), lambda i, j: (l, i, 0, 0)),
            _layer_spec(g1, l),
            pl.BlockSpec((1, ts, BRANCH_W), tile),
            pl.BlockSpec((1, ts, BRANCH_W), tile),
            pl.BlockSpec((1, ts, BRANCH_W), tile),
            _layer_spec(wbr, l),
            _layer_spec(wmg, l),
            _layer_spec(bmg, l),
            _layer_spec(wo, l),
        ],
        out_specs=pl.BlockSpec((1, ts, d), tile),
        compiler_params=_params("arbitrary", "arbitrary"),
        name="out_proj",
    )(x, mod4, g1, ret, gla, fox, wbr, wmg, bmg, wo)


def _ffn_kernel(x_ref, mod_ref, g2_ref, wup_ref, wconv_ref, bconv_ref, wdown_ref, o_ref, tail_ref):
    d = x_ref.shape[2]
    ts = x_ref.shape[1]
    ff = wdown_ref.shape[1]

    @pl.when(pl.program_id(1) == 0)
    def _():
        tail_ref[...] = jnp.zeros_like(tail_ref)

    x = x_ref[0]
    mod = mod_ref[0, 0]
    hb = _norm_modulate(x, g2_ref[0], mod[:, 3 * d:4 * d], mod[:, 4 * d:5 * d]).astype(BF16)
    u = _dot(hb, wup_ref[0, :, 0:ff])
    g = _dot(hb, wup_ref[0, :, ff:2 * ff])
    row = lax.broadcasted_iota(jnp.int32, u.shape, 0)
    prev1 = tail_ref[CONV_W - 2:CONV_W - 1, :]
    prev2 = tail_ref[CONV_W - 3:CONV_W - 2, :]
    u1 = jnp.where(row == 0, prev1, pltpu.roll(u, 1, axis=0))
    u2 = jnp.where(row == 0, prev2, jnp.where(row == 1, prev1, pltpu.roll(u, 2, axis=0)))
    tail_ref[...] = u[ts - (CONV_W - 1):ts, :]
    wc = wconv_ref[0]
    conv = bconv_ref[0] + wc[0:1, :] * u2 + wc[1:2, :] * u1 + wc[2:3, :] * u
    act = (_silu(conv) * g).astype(BF16)
    y = _dot(act, wdown_ref[0])
    o_ref[0] = x + mod[:, 5 * d:6 * d] * y


def _ffn_call(l, x, mod4, g2, wup, wconv, bconv, wdown):
    b, s, d = x.shape
    ts = min(TS_FFN, s)
    ff = wdown.shape[1]
    tile = lambda i, j: (i, j, 0)
    return pl.pallas_call(
        _ffn_kernel,
        out_shape=jax.ShapeDtypeStruct((b, s, d), F32),
        grid=(b, s // ts),
        in_specs=[
            pl.BlockSpec((1, ts, d), tile),
            pl.BlockSpec((1, 1, 1, mod4.shape[3]), lambda i, j: (l, i, 0, 0)),
            _layer_spec(g2, l),
            _layer_spec(wup, l),
            _layer_spec(wconv, l),
            _layer_spec(bconv, l),
            _layer_spec(wdown, l),
        ],
        out_specs=pl.BlockSpec((1, ts, d), tile),
        scratch_shapes=[pltpu.VMEM((CONV_W - 1, ff), F32)],
        compiler_params=_params("arbitrary", "arbitrary"),
        name="conv_ffn",
    )(x, mod4, g2, wup, wconv, bconv, wdown)


def _rope_tables(s):
    half = HEAD_DIM // 2
    pos = jnp.arange(s, dtype=F32)
    inv_freq = ROPE_BASE ** (-jnp.arange(half, dtype=F32) / half)
    ang = pos[:, None] * inv_freq[None, :]
    cos, sin = jnp.cos(ang), jnp.sin(ang)
    return jnp.concatenate([cos, cos], axis=1), jnp.concatenate([-sin, sin], axis=1)


def _retention_tables(ts):
    log_g = jnp.log1p(-jnp.exp2(-5.0 - jnp.arange(N_HEADS, dtype=F32)))[:, None, None]
    t = jnp.arange(ts, dtype=F32)
    diff = t[:, None] - t[None, :]
    same = (jnp.arange(ts)[:, None] // CHUNK) == (jnp.arange(ts)[None, :] // CHUNK)
    expo = jnp.where(same, jnp.abs(diff), diff)[None]
    mask = jnp.where((same | (diff > 0))[None], jnp.exp(jnp.where(expo >= 0, expo, 0.0) * log_g), 0.0)
    ones = jnp.ones((1, 1, HEAD_DIM), F32)
    qw = jnp.exp((t + 1.0)[None, :, None] * log_g) * ones
    kw = jnp.exp((ts - 1.0 - t)[None, :, None] * log_g) * ones
    gts = jnp.exp(ts * log_g) * ones
    return mask, qw, kw, gts


def _chunk_matrices(ts):
    i = jnp.arange(ts)
    same = (i[:, None] // CHUNK) == (i[None, :] // CHUNK)
    btri = (same & (i[:, None] >= i[None, :])).astype(BF16)
    return btri, same.astype(BF16)


def kernel(x, c, norm1_g, norm2_g, w_ada, b_ada, w_in, w_gla_a2, b_gla_a, b_fox_f, ret_norm_g, gla_norm_g,
           q_norm_g, k_norm_g, w_br, w_mg, b_mg, w_o, w_up, w_conv, b_conv, w_down):
    b, s, d = x.shape
    depth = w_ada.shape[0]
    row = lambda a: a[:, None, :]

    mod4 = _ada_call(c, w_ada, b_ada)[:, :, None, :]
    wmain, wsmall = _win_prep_call(w_in)
    wa2 = jnp.pad(w_gla_a2, ((0, 0), (0, LANES - GLA_LOWRANK), (0, 0))).astype(BF16)
    bfox = jnp.pad(b_fox_f, ((0, 0), (FF_LANE0, LANES - FF_LANE0 - N_HEADS)))
    wbr, wmg, wo, wup, wdown = (a.astype(BF16) for a in (w_br, w_mg, w_o, w_up, w_down))
    g1, g2, ba, bfox, rg, gg, qg, kg, bmg, bconv = (
        row(a) for a in (norm1_g, norm2_g, b_gla_a, bfox, ret_norm_g, gla_norm_g, q_norm_g, k_norm_g,
                         b_mg, b_conv))

    ts_in = min(TS_IN, s)
    ts_mix = min(TS_MIX, s)
    i = jnp.arange(ts_in)
    tri = (i[:, None] >= i[None, :]).astype(BF16)
    sel = (jnp.arange(LANES)[None, :] == FF_LANE0 + jnp.arange(SUBLANES)[:, None]).astype(BF16)
    sel = sel * (jnp.arange(SUBLANES)[:, None] < N_HEADS).astype(BF16)
    cos_t, sin_t = _rope_tables(s)
    mret, qw, kw, gts = _retention_tables(ts_mix)
    btri, bones = _chunk_matrices(ts_mix)

    for l in range(depth):
        r, gl, f, la, cum, crow = _inproj_call(
            l, x, mod4, g1, wmain, wsmall, wa2, ba, bfox, cos_t, sin_t, qg, kg, tri, sel)
        ret, gla_o = _mixer_call(l, r, gl, la, mret, qw, kw, gts, btri, bones, rg, gg)
        fox = _fox_call(f, cum, crow)
        x = _outproj_call(l, x, mod4, g1, ret, gla_o, fox, wbr, wmg, bmg, wo)
        x = _ffn_call(l, x, mod4, g2, wup, w_conv, bconv, wdown)
    return x
```

```python
import jax
import jax.numpy as jnp
from jax import lax
from jax.experimental import pallas as pl
from jax.experimental.pallas import tpu as pltpu

F32 = jnp.float32
BF16 = jnp.bfloat16

CHUNK = 64
N_HEADS = 4
HEAD_DIM = 128
BRANCH_W = N_HEADS * HEAD_DIM
GLA_DK = HEAD_DIM // 2
GLA_LOWRANK = 16
GLA_TAU = 16.0
N_BRANCH = 3
CONV_W = 3
ROPE_BASE = 10000.0
EPS = 1e-6

LANES = 128
SUBLANES = 8
FF_LANE0 = GLA_LOWRANK
AUG_ROWMAX0 = 6
LOG2E = 1.4426950408889634

VMEM_LIMIT = 56 * 1024 * 1024

TS_IN = 256
TS_MIX = 256
TS_OUT = 256
TS_FFN = 256
TR_PREP = 256

_NK = N_HEADS * GLA_DK
O_GLR = 4 * BRANCH_W + 2 * _NK + BRANCH_W
O_GG = O_GLR + GLA_LOWRANK
O_FF = O_GG + 4 * BRANCH_W
W_MAIN = O_GLR + (O_FF - O_GG)


def _dot(a, b):
    return jnp.dot(a, b, preferred_element_type=F32)


def _dot_nt(a, b):
    return lax.dot_general(a, b, (((1,), (1,)), ((), ())), preferred_element_type=F32)


def _dot_tn(a, b):
    return lax.dot_general(a, b, (((0,), (0,)), ((), ())), preferred_element_type=F32)


def _split3(x):
    x_hi = x.astype(BF16).astype(F32)
    r1 = x - x_hi
    x_mid = r1.astype(BF16).astype(F32)
    x_lo = (r1 - x_mid).astype(BF16).astype(F32)
    return x_hi, x_mid, x_lo


def _split3_dot(m_bf16, x):
    x_hi, x_mid, x_lo = (t.astype(BF16) for t in _split3(x))
    return _dot(m_bf16, x_hi) + _dot(m_bf16, x_mid) + _dot(m_bf16, x_lo)


def _log_sigmoid(x):
    return jnp.minimum(x, 0.0) - jnp.log1p(jnp.exp(-jnp.abs(x)))


def _sigmoid(x):
    return 1.0 / (1.0 + jnp.exp(-x))


def _silu(x):
    return x * _sigmoid(x)


def _norm_modulate(x, g, shift, scale):
    ms = jnp.mean(x * x, axis=-1, keepdims=True)
    y = x * lax.rsqrt(ms + EPS) * g
    return y * (1.0 + scale) + shift


def _params(*sem):
    return pltpu.CompilerParams(dimension_semantics=sem, vmem_limit_bytes=VMEM_LIMIT)


def _layer_spec(arr, l):
    zeros = (0,) * (arr.ndim - 1)
    return pl.BlockSpec((1,) + arr.shape[1:], lambda *_: (l,) + zeros)


def _const_spec(arr):
    zeros = (0,) * arr.ndim
    return pl.BlockSpec(arr.shape, lambda *_: zeros)


def _ada_kernel(c_ref, w_ref, b_ref, o_ref):
    c_act = _silu(c_ref[...]).astype(BF16)
    o_ref[0] = _dot(c_act, w_ref[0].astype(BF16)) + b_ref[0]


def _ada_call(c, w_ada, b_ada):
    depth, d, n = w_ada.shape
    b = c.shape[0]
    tn = n // 4
    return pl.pallas_call(
        _ada_kernel,
        out_shape=jax.ShapeDtypeStruct((depth, b, n), F32),
        grid=(depth, n // tn),
        in_specs=[
            pl.BlockSpec((b, d), lambda l, j: (0, 0)),
            pl.BlockSpec((1, d, tn), lambda l, j: (l, 0, j)),
            pl.BlockSpec((1, 1, tn), lambda l, j: (l, 0, j)),
        ],
        out_specs=pl.BlockSpec((1, b, tn), lambda l, j: (l, 0, j)),
        compiler_params=_params("arbitrary", "arbitrary"),
        name="ada_mod",
    )(c, w_ada, b_ada.reshape(depth, 1, n))


def _win_prep_kernel(w_ref, main_ref, small_ref):
    tr = w_ref.shape[1]
    main_ref[0, :, 0:O_GLR] = w_ref[0, :, 0:O_GLR].astype(BF16)
    main_ref[0, :, O_GLR:W_MAIN] = w_ref[0, :, O_GG:O_FF].astype(BF16)
    small = jnp.concatenate(
        [w_ref[0, :, O_GLR:O_GG], w_ref[0, :, O_FF:O_FF + N_HEADS],
         jnp.zeros((tr, LANES - GLA_LOWRANK - N_HEADS), F32)], axis=1)
    small_ref[0] = small.astype(BF16)


def _win_prep_call(w_in):
    depth, d, n = w_in.shape
    tr = TR_PREP
    tile = lambda l, i: (l, i, 0)
    return pl.pallas_call(
        _win_prep_kernel,
        out_shape=(jax.ShapeDtypeStruct((depth, d, W_MAIN), BF16),
                   jax.ShapeDtypeStruct((depth, d, LANES), BF16)),
        grid=(depth, d // tr),
        in_specs=[pl.BlockSpec((1, tr, n), tile)],
        out_specs=(pl.BlockSpec((1, tr, W_MAIN), tile), pl.BlockSpec((1, tr, LANES), tile)),
        compiler_params=_params("arbitrary", "arbitrary"),
        name="win_prep",
    )(w_in)


def _inproj_kernel(x_ref, mod_ref, g1_ref, wmain_ref, wsmall_ref, wa2_ref, ba_ref, bfox_ref,
                   cos_ref, sin_ref, qg_ref, kg_ref, tri_ref,
                   r_ref, gl_ref, fq_ref, fk_ref, fv_ref, la_ref, carry_ref):
    d = x_ref.shape[2]
    w = BRANCH_W

    @pl.when(pl.program_id(1) == 0)
    def _():
        carry_ref[...] = jnp.zeros_like(carry_ref)

    x = x_ref[0]
    mod = mod_ref[0, 0]
    h = _norm_modulate(x, g1_ref[0], mod[:, 0:d], mod[:, d:2 * d])
    hb = h.astype(BF16)
    cos = cos_ref[...]
    sin = sin_ref[...]

    def heads_map(acc, fn):
        return jnp.concatenate(
            [fn(acc[:, i * HEAD_DIM:(i + 1) * HEAD_DIM]) for i in range(N_HEADS)], axis=1)

    def rope(t):
        return t * cos + pltpu.roll(t, HEAD_DIM // 2, axis=1) * sin

    def qk_norm(g):
        def fn(t):
            ms = jnp.mean(t * t, axis=-1, keepdims=True)
            return t * lax.rsqrt(ms + EPS) * g
        return fn

    def proj(c0, n):
        return _dot(hb, wmain_ref[0, :, c0:c0 + n])

    r_ref[0, :, 0:w] = heads_map(proj(0, w), rope).astype(BF16)
    r_ref[0, :, w:2 * w] = (heads_map(proj(w, w), rope) * HEAD_DIM ** -0.5).astype(BF16)
    r_ref[0, :, 2 * w:4 * w] = proj(2 * w, 2 * w).astype(BF16)
    c0 = 4 * w
    gqk = proj(c0, w)
    lane = lax.broadcasted_iota(jnp.int32, gqk.shape, 1)
    gqk = jnp.where(lane < _NK, gqk * GLA_DK ** -0.5, gqk)
    gl_ref[0, :, 0:w] = gqk.astype(BF16)
    gl_ref[0, :, w:3 * w] = proj(c0 + w, 2 * w).astype(BF16)
    small = _dot(hb, wsmall_ref[0])
    la_pre = _dot(small.astype(BF16), wa2_ref[0]) + ba_ref[0]
    la_ref[0] = _log_sigmoid(la_pre) * (1.0 / GLA_TAU)
    lane = lax.broadcasted_iota(jnp.int32, small.shape, 1)
    is_ff = (lane >= FF_LANE0) & (lane < FF_LANE0 + N_HEADS)
    log_f = jnp.where(is_ff, _log_sigmoid(small + bfox_ref[0]), 0.0)
    cum = _split3_dot(tri_ref[...], log_f) + carry_ref[...]
    carry_ref[...] = cum[cum.shape[0] - 1:cum.shape[0], :]

    c0 = 7 * w
    fq, fk, fv = proj(c0, w), proj(c0 + w, w), proj(c0 + 2 * w, w)
    q_norm, k_norm = qk_norm(qg_ref[0]), qk_norm(kg_ref[0])
    q_ones = jnp.where((lane >= 3) & (lane < AUG_ROWMAX0), 1.0, 0.0)
    k_ones = jnp.where((lane < 3) | ((lane >= AUG_ROWMAX0) & (lane < AUG_ROWMAX0 + 3)), 1.0, 0.0)
    for i in range(N_HEADS):
        sl = slice(i * HEAD_DIM, (i + 1) * HEAD_DIM)
        o = 2 * i * HEAD_DIM
        c = jnp.broadcast_to(cum[:, FF_LANE0 + i:FF_LANE0 + i + 1] * LOG2E, (cum.shape[0], LANES))
        c_hi, c_mid, c_lo = _split3(c)
        q_extra = jnp.where(lane == 0, c_hi, jnp.where(lane == 1, c_mid, jnp.where(lane == 2, c_lo, q_ones)))
        k_extra = jnp.where(lane == 3, -c_hi, jnp.where(lane == 4, -c_mid, jnp.where(lane == 5, -c_lo, k_ones)))
        fq_ref[0, :, o:o + HEAD_DIM] = (q_norm(fq[:, sl]) * (HEAD_DIM ** -0.5 * LOG2E)).astype(BF16)
        fq_ref[0, :, o + HEAD_DIM:o + 2 * HEAD_DIM] = q_extra.astype(BF16)
        fk_ref[0, :, o:o + HEAD_DIM] = k_norm(fk[:, sl]).astype(BF16)
        fk_ref[0, :, o + HEAD_DIM:o + 2 * HEAD_DIM] = k_extra.astype(BF16)
        fv_ref[0, :, o:o + HEAD_DIM] = fv[:, sl].astype(BF16)
        fv_ref[0, :, o + HEAD_DIM:o + 2 * HEAD_DIM] = jnp.ones((cum.shape[0], HEAD_DIM), BF16)


def _inproj_call(l, x, mod4, g1, wmain, wsmall, wa2, ba, bfox, cos_t, sin_t, qg, kg, tri):
    b, s, d = x.shape
    ts = min(TS_IN, s)
    w = BRANCH_W
    tile = lambda i, j: (i, j, 0)
    return pl.pallas_call(
        _inproj_kernel,
        out_shape=(
            jax.ShapeDtypeStruct((b, s, 4 * w), BF16),
            jax.ShapeDtypeStruct((b, s, 3 * w), BF16),
            jax.ShapeDtypeStruct((b, s, 2 * w), BF16),
            jax.ShapeDtypeStruct((b, s, 2 * w), BF16),
            jax.ShapeDtypeStruct((b, s, 2 * w), BF16),
            jax.ShapeDtypeStruct((b, s, _NK), F32),
        ),
        grid=(b, s // ts),
        in_specs=[
            pl.BlockSpec((1, ts, d), tile),
            pl.BlockSpec((1, 1, 1, mod4.shape[3]), lambda i, j: (l, i, 0, 0)),
            _layer_spec(g1, l),
            _layer_spec(wmain, l),
            _layer_spec(wsmall, l),
            _layer_spec(wa2, l),
            _layer_spec(ba, l),
            _layer_spec(bfox, l),
            pl.BlockSpec((ts, HEAD_DIM), lambda i, j: (j, 0)),
            pl.BlockSpec((ts, HEAD_DIM), lambda i, j: (j, 0)),
            _layer_spec(qg, l),
            _layer_spec(kg, l),
            _const_spec(tri),
        ],
        out_specs=(
            pl.BlockSpec((1, ts, 4 * w), tile),
            pl.BlockSpec((1, ts, 3 * w), tile),
            pl.BlockSpec((1, ts, 2 * w), tile),
            pl.BlockSpec((1, ts, 2 * w), tile),
            pl.BlockSpec((1, ts, 2 * w), tile),
            pl.BlockSpec((1, ts, _NK), tile),
        ),
        scratch_shapes=[pltpu.VMEM((1, LANES), F32)],
        compiler_params=_params("arbitrary", "arbitrary"),
        name="in_proj",
    )(x, mod4, g1, wmain, wsmall, wa2, ba, bfox, cos_t, sin_t, qg, kg, tri)


def _mixer_kernel(r_ref, gl_ref, la_ref, mret_ref, qw_ref, kw_ref, gts_ref, btri_ref, bones_ref,
                  rg_ref, gg_ref, ret_ref, gla_ref, rstate_ref, gstate_ref):
    w = BRANCH_W
    ts = r_ref.shape[1]

    @pl.when(pl.program_id(1) == 0)
    def _():
        rstate_ref[...] = jnp.zeros_like(rstate_ref)
        gstate_ref[...] = jnp.zeros_like(gstate_ref)

    for h in range(N_HEADS):
        sl = slice(h * HEAD_DIM, (h + 1) * HEAD_DIM)
        q = r_ref[0, :, sl]
        k = r_ref[0, :, w + h * HEAD_DIM:w + (h + 1) * HEAD_DIM]
        v = r_ref[0, :, 2 * w + h * HEAD_DIM:2 * w + (h + 1) * HEAD_DIM]
        gate = r_ref[0, :, 3 * w + h * HEAD_DIM:3 * w + (h + 1) * HEAD_DIM].astype(F32)
        state = rstate_ref[h]
        p = (_dot_nt(q, k) * mret_ref[h]).astype(BF16)
        o = _dot(p, v) + _dot(q, state.astype(BF16)) * qw_ref[h]
        kk = (k.astype(F32) * kw_ref[h]).astype(BF16)
        rstate_ref[h] = gts_ref[h] * state + _dot_tn(kk, v)
        mu = jnp.mean(o, axis=-1, keepdims=True)
        oc = o - mu
        var = jnp.mean(oc * oc, axis=-1, keepdims=True)
        on = oc * lax.rsqrt(var + EPS) * rg_ref[0, :, sl]
        ret_ref[0, :, sl] = (_silu(gate) * on).astype(BF16)

    la = la_ref[0]
    b_cum = _split3_dot(btri_ref[...], la)
    b_end = _split3_dot(bones_ref[...], la)
    kdec = jnp.exp(b_end - b_cum)
    a_all = jnp.exp(b_end)
    kk_all = (gl_ref[0, :, _NK:2 * _NK].astype(F32) * kdec).astype(BF16)
    q_all = gl_ref[0, :, 0:_NK]
    lane = lax.broadcasted_iota(jnp.int32, (ts, LANES), 1)
    chunks = [slice(c * CHUNK, (c + 1) * CHUNK) for c in range(ts // CHUNK)]
    pair_lanes = [slice((h // 2) * LANES, (h // 2 + 1) * LANES) for h in range(N_HEADS)]
    kvs = []
    for h in range(N_HEADS):
        kk = kk_all[:, pair_lanes[h]]
        v = gl_ref[0, :, w + h * HEAD_DIM:w + (h + 1) * HEAD_DIM]
        kvs.append([_dot_tn(v[rows], kk[rows]) for rows in chunks])
    states = []
    for h in range(N_HEADS):
        state = gstate_ref[h]
        per_chunk = []
        for c, rows in enumerate(chunks):
            state = a_all[c * CHUNK:c * CHUNK + 1, pair_lanes[h]] * state + kvs[h][c]
            per_chunk.append(state.astype(BF16))
        gstate_ref[h] = state
        states.append(per_chunk)
    for h in range(N_HEADS):
        own = (lane >= (h % 2) * GLA_DK) & (lane < (h % 2 + 1) * GLA_DK)
        qm = jnp.where(own, q_all[:, pair_lanes[h]], jnp.zeros((), BF16))
        gate = gl_ref[0, :, 2 * w + h * HEAD_DIM:2 * w + (h + 1) * HEAD_DIM].astype(F32)
        o = jnp.concatenate([_dot_nt(qm[rows], states[h][c]) for c, rows in enumerate(chunks)], axis=0)
        ms = jnp.mean(o * o, axis=-1, keepdims=True)
        on = o * lax.rsqrt(ms + EPS) * gg_ref[0]
        gla_ref[0, :, h * HEAD_DIM:(h + 1) * HEAD_DIM] = (_silu(gate) * on).astype(BF16)


def _mixer_call(l, r, gl, la, mret, qw, kw, gts, btri, bones, rg, gg):
    b, s, _ = r.shape
    ts = mret.shape[1]
    w = BRANCH_W
    tile = lambda i, j: (i, j, 0)
    return pl.pallas_call(
        _mixer_kernel,
        out_shape=(jax.ShapeDtypeStruct((b, s, w), BF16), jax.ShapeDtypeStruct((b, s, w), BF16)),
        grid=(b, s // ts),
        in_specs=[
            pl.BlockSpec((1, ts, 4 * w), tile),
            pl.BlockSpec((1, ts, 3 * w), tile),
            pl.BlockSpec((1, ts, _NK), tile),
            _const_spec(mret),
            _const_spec(qw),
            _const_spec(kw),
            _const_spec(gts),
            _const_spec(btri),
            _const_spec(bones),
            _layer_spec(rg, l),
            _layer_spec(gg, l),
        ],
        out_specs=(pl.BlockSpec((1, ts, w), tile), pl.BlockSpec((1, ts, w), tile)),
        scratch_shapes=[pltpu.VMEM((N_HEADS, HEAD_DIM, HEAD_DIM), F32),
                        pltpu.VMEM((N_HEADS, HEAD_DIM, LANES), F32)],
        compiler_params=_params("arbitrary", "arbitrary"),
        name="ret_gla_mixer",
    )(r, gl, la, mret, qw, kw, gts, btri, bones, rg, gg)


def _fox_kernel(q_ref, k_ref, v_ref, o_ref, mpart_ref, q2_ref, acc_ref):
    tq = q_ref.shape[1]
    tk = 2 * tq
    qi = pl.program_id(1)
    hw = 2 * HEAD_DIM
    hs = [slice(h * hw, (h + 1) * hw) for h in range(N_HEADS)]
    n_full = qi // 2
    row = lax.broadcasted_iota(jnp.int32, (tq, tk), 0)
    col = lax.broadcasted_iota(jnp.int32, (tq, tk), 1)
    causal = col <= row + tq * (qi % 2)

    def key_rows(jb):
        return pl.ds(pl.multiple_of(jb * tk, tk), tk)

    def scores(lhs, jb, mask):
        rows = key_rows(jb)
        out = [_dot_nt(lhs(h), k_ref[0, rows, hs[h]]) for h in range(N_HEADS)]
        if mask is not None:
            out = [jnp.where(mask, s, -jnp.inf) for s in out]
        return out

    def pass1(jb, mask):
        for h, s in enumerate(scores(lambda h: q_ref[0, :, hs[h]], jb, mask)):
            m = mpart_ref[h]
            for t in range(tk // LANES):
                m = jnp.maximum(m, s[:, t * LANES:(t + 1) * LANES])
            mpart_ref[h] = m

    def pass2(jb, mask):
        rows = key_rows(jb)
        ps = [jnp.exp2(s).astype(BF16) for s in scores(lambda h: q2_ref[h], jb, mask)]
        for h in range(N_HEADS):
            acc_ref[h] += _dot(ps[h], v_ref[0, rows, hs[h]])

    def loop(fn):
        def body(jb, carry):
            fn(jb, None)
            return carry
        lax.fori_loop(0, n_full, body, 0)
        fn(n_full, causal)

    mpart_ref[...] = jnp.full(mpart_ref.shape, -jnp.inf, F32)
    loop(pass1)
    lane = lax.broadcasted_iota(jnp.int32, (tq, LANES), 1)
    for h in range(N_HEADS):
        m = jnp.broadcast_to(jnp.max(mpart_ref[h], axis=-1, keepdims=True), (tq, LANES))
        m_hi, m_mid, m_lo = _split3(m)
        extra = q_ref[0, :, h * hw + HEAD_DIM:(h + 1) * hw].astype(F32)
        extra = jnp.where(lane == AUG_ROWMAX0, -m_hi, jnp.where(
            lane == AUG_ROWMAX0 + 1, -m_mid, jnp.where(lane == AUG_ROWMAX0 + 2, -m_lo, extra)))
        q2_ref[h, :, 0:HEAD_DIM] = q_ref[0, :, h * hw:h * hw + HEAD_DIM]
        q2_ref[h, :, HEAD_DIM:hw] = extra.astype(BF16)
    acc_ref[...] = jnp.zeros_like(acc_ref)
    loop(pass2)
    for h in range(N_HEADS):
        acc = acc_ref[h]
        o_ref[0, :, h * HEAD_DIM:(h + 1) * HEAD_DIM] = (acc[:, 0:HEAD_DIM] / acc[:, HEAD_DIM:hw]).astype(BF16)


def _fox_call(fq, fk, fv, tq):
    b, s, wide = fq.shape
    assert s % (2 * tq) == 0
    return pl.pallas_call(
        _fox_kernel,
        out_shape=jax.ShapeDtypeStruct((b, s, BRANCH_W), BF16),
        grid=(b, s // tq),
        in_specs=[
            pl.BlockSpec((1, tq, wide), lambda i, j: (i, j, 0)),
            pl.BlockSpec((1, s, wide), lambda i, j: (i, 0, 0)),
            pl.BlockSpec((1, s, wide), lambda i, j: (i, 0, 0)),
        ],
        out_specs=pl.BlockSpec((1, tq, BRANCH_W), lambda i, j: (i, j, 0)),
        scratch_shapes=[pltpu.VMEM((N_HEADS, tq, LANES), F32),
                        pltpu.VMEM((N_HEADS, tq, 2 * HEAD_DIM), BF16),
                        pltpu.VMEM((N_HEADS, tq, 2 * HEAD_DIM), F32)],
        compiler_params=_params("arbitrary", "arbitrary"),
        name="fox_attention",
    )(fq, fk, fv)


def _outproj_kernel(x_ref, mod_ref, g1_ref, ret_ref, gla_ref, fox_ref, wbr_ref, wmg_ref, bmg_ref, wo_ref,
                    o_ref):
    d = x_ref.shape[2]
    x = x_ref[0]
    mod = mod_ref[0, 0]
    hb = _norm_modulate(x, g1_ref[0], mod[:, 0:d], mod[:, d:2 * d]).astype(BF16)
    mixed = None
    for n, br_ref in enumerate((ret_ref, gla_ref, fox_ref)):
        y = _dot(br_ref[0], wbr_ref[0, n])
        gate = _sigmoid(_dot(hb, wmg_ref[0, :, n * d:(n + 1) * d]) + bmg_ref[0, :, n * d:(n + 1) * d])
        mixed = gate * y if mixed is None else mixed + gate * y
    out = _dot(mixed.astype(BF16), wo_ref[0])
    o_ref[0] = x + mod[:, 2 * d:3 * d] * out


def _outproj_call(l, x, mod4, g1, ret, gla, fox, wbr, wmg, bmg, wo):
    b, s, d = x.shape
    ts = min(TS_OUT, s)
    tile = lambda i, j: (i, j, 0)
    return pl.pallas_call(
        _outproj_kernel,
        out_shape=jax.ShapeDtypeStruct((b, s, d), F32),
        grid=(b, s // ts),
        in_specs=[
            pl.BlockSpec((1, ts, d), tile),
            pl.BlockSpec((1, 1, 1, mod4.shape[3]), lambda i, j: (l, i, 0, 0)),
            _layer_spec(g1, l),
            pl.BlockSpec((1, ts, BRANCH_W), tile),
            pl.BlockSpec((1, ts, BRANCH_W), tile),
            pl.BlockSpec((1, ts, BRANCH_W), tile),
            _layer_spec(wbr, l),
            _layer_spec(wmg, l),
            _layer_spec(bmg, l),
            _layer_spec(wo, l),
        ],
        out_specs=pl.BlockSpec((1, ts, d), tile),
        compiler_params=_params("arbitrary", "arbitrary"),
        name="out_proj",
    )(x, mod4, g1, ret, gla, fox, wbr, wmg, bmg, wo)


def _ffn_kernel(x_ref, mod_ref, g2_ref, wup_ref, wconv_ref, bconv_ref, wdown_ref, o_ref, tail_ref):
    d = x_ref.shape[2]
    ts = x_ref.shape[1]
    ff = wdown_ref.shape[1]

    @pl.when(pl.program_id(1) == 0)
    def _():
        tail_ref[...] = jnp.zeros_like(tail_ref)

    x = x_ref[0]
    mod = mod_ref[0, 0]
    hb = _norm_modulate(x, g2_ref[0], mod[:, 3 * d:4 * d], mod[:, 4 * d:5 * d]).astype(BF16)
    u = _dot(hb, wup_ref[0, :, 0:ff])
    g = _dot(hb, wup_ref[0, :, ff:2 * ff])
    row = lax.broadcasted_iota(jnp.int32, u.shape, 0)
    prev1 = tail_ref[CONV_W - 2:CONV_W - 1, :]
    prev2 = tail_ref[CONV_W - 3:CONV_W - 2, :]
    u1 = jnp.where(row == 0, prev1, pltpu.roll(u, 1, axis=0))
    u2 = jnp.where(row == 0, prev2, jnp.where(row == 1, prev1, pltpu.roll(u, 2, axis=0)))
    tail_ref[...] = u[ts - (CONV_W - 1):ts, :]
    wc = wconv_ref[0]
    conv = bconv_ref[0] + wc[0:1, :] * u2 + wc[1:2, :] * u1 + wc[2:3, :] * u
    act = (_silu(conv) * g).astype(BF16)
    y = _dot(act, wdown_ref[0])
    o_ref[0] = x + mod[:, 5 * d:6 * d] * y


def _ffn_call(l, x, mod4, g2, wup, wconv, bconv, wdown):
    b, s, d = x.shape
    ts = min(TS_FFN, s)
    ff = wdown.shape[1]
    tile = lambda i, j: (i, j, 0)
    return pl.pallas_call(
        _ffn_kernel,
        out_shape=jax.ShapeDtypeStruct((b, s, d), F32),
        grid=(b, s // ts),
        in_specs=[
            pl.BlockSpec((1, ts, d), tile),
            pl.BlockSpec((1, 1, 1, mod4.shape[3]), lambda i, j: (l, i, 0, 0)),
            _layer_spec(g2, l),
            _layer_spec(wup, l),
            _layer_spec(wconv, l),
            _layer_spec(bconv, l),
            _layer_spec(wdown, l),
        ],
        out_specs=pl.BlockSpec((1, ts, d), tile),
        scratch_shapes=[pltpu.VMEM((CONV_W - 1, ff), F32)],
        compiler_params=_params("arbitrary", "arbitrary"),
        name="conv_ffn",
    )(x, mod4, g2, wup, wconv, bconv, wdown)


def _rope_tables(s):
    half = HEAD_DIM // 2
    pos = jnp.arange(s, dtype=F32)
    inv_freq = ROPE_BASE ** (-jnp.arange(half, dtype=F32) / half)
    ang = pos[:, None] * inv_freq[None, :]
    cos, sin = jnp.cos(ang), jnp.sin(ang)
    return jnp.concatenate([cos, cos], axis=1), jnp.concatenate([-sin, sin], axis=1)


def _retention_tables(ts):
    log_g = jnp.log1p(-jnp.exp2(-5.0 - jnp.arange(N_HEADS, dtype=F32)))[:, None, None]
    t = jnp.arange(ts, dtype=F32)
    diff = t[:, None] - t[None, :]
    same = (jnp.arange(ts)[:, None] // CHUNK) == (jnp.arange(ts)[None, :] // CHUNK)
    expo = jnp.where(same, jnp.abs(diff), diff)[None]
    mask = jnp.where((same | (diff > 0))[None], jnp.exp(jnp.where(expo >= 0, expo, 0.0) * log_g), 0.0)
    ones = jnp.ones((1, 1, HEAD_DIM), F32)
    qw = jnp.exp((t + 1.0)[None, :, None] * log_g) * ones
    kw = jnp.exp((ts - 1.0 - t)[None, :, None] * log_g) * ones
    gts = jnp.exp(ts * log_g) * ones
    return mask, qw, kw, gts


def _chunk_matrices(ts):
    i = jnp.arange(ts)
    same = (i[:, None] // CHUNK) == (i[None, :] // CHUNK)
    btri = (same & (i[:, None] >= i[None, :])).astype(BF16)
    return btri, same.astype(BF16)


def kernel(x, c, norm1_g, norm2_g, w_ada, b_ada, w_in, w_gla_a2, b_gla_a, b_fox_f, ret_norm_g, gla_norm_g,
           q_norm_g, k_norm_g, w_br, w_mg, b_mg, w_o, w_up, w_conv, b_conv, w_down):
    b, s, d = x.shape
    depth = w_ada.shape[0]
    row = lambda a: a[:, None, :]

    mod4 = _ada_call(c, w_ada, b_ada)[:, :, None, :]
    wmain, wsmall = _win_prep_call(w_in)
    wa2 = jnp.pad(w_gla_a2, ((0, 0), (0, LANES - GLA_LOWRANK), (0, 0))).astype(BF16)
    bfox = jnp.pad(b_fox_f, ((0, 0), (FF_LANE0, LANES - FF_LANE0 - N_HEADS)))
    wbr, wmg, wo, wup, wdown = (a.astype(BF16) for a in (w_br, w_mg, w_o, w_up, w_down))
    g1, g2, ba, bfox, rg, gg, qg, kg, bmg, bconv = (
        row(a) for a in (norm1_g, norm2_g, b_gla_a, bfox, ret_norm_g, gla_norm_g, q_norm_g, k_norm_g,
                         b_mg, b_conv))

    ts_in = min(TS_IN, s)
    ts_mix = min(TS_MIX, s)
    i = jnp.arange(ts_in)
    tri = (i[:, None] >= i[None, :]).astype(BF16)
    cos_t, sin_t = _rope_tables(s)
    mret, qw, kw, gts = _retention_tables(ts_mix)
    btri, bones = _chunk_matrices(ts_mix)

    for l in range(depth):
        r, gl, fq, fk, fv, la = _inproj_call(
            l, x, mod4, g1, wmain, wsmall, wa2, ba, bfox, cos_t, sin_t, qg, kg, tri)
        ret, gla_o = _mixer_call(l, r, gl, la, mret, qw, kw, gts, btri, bones, rg, gg)
        fox = _fox_call(fq, fk, fv, ts_in)
        x = _outproj_call(l, x, mod4, g1, ret, gla_o, fox, wbr, wmg, bmg, wo)
        x = _ffn_call(l, x, mod4, g2, wup, w_conv, bconv, wdown)
    return x
```

```python
import jax
import jax.numpy as jnp
from jax import lax
from jax.experimental import pallas as pl
from jax.experimental.pallas import tpu as pltpu

F32 = jnp.float32
BF16 = jnp.bfloat16

CHUNK = 64
N_HEADS = 4
HEAD_DIM = 128
BRANCH_W = N_HEADS * HEAD_DIM
GLA_DK = HEAD_DIM // 2
GLA_LOWRANK = 16
GLA_TAU = 16.0
N_BRANCH = 3
CONV_W = 3
ROPE_BASE = 10000.0
EPS = 1e-6

LANES = 128
SUBLANES = 8
FF_LANE0 = GLA_LOWRANK
AUG_ROWMAX0 = 6
LOG2E = 1.4426950408889634

VMEM_LIMIT = 56 * 1024 * 1024

TS_IN = 512
TS_MIX = 256
TQ_FOX = 512
TS_OUT = 512
TS_FFN = 512
TR_PREP = 256

_NK = N_HEADS * GLA_DK
O_GLR = 4 * BRANCH_W + 2 * _NK + BRANCH_W
O_GG = O_GLR + GLA_LOWRANK
O_FF = O_GG + 4 * BRANCH_W
W_MAIN = O_GLR + (O_FF - O_GG)


def _dot(a, b):
    return jnp.dot(a, b, preferred_element_type=F32)


def _dot_nt(a, b):
    return lax.dot_general(a, b, (((1,), (1,)), ((), ())), preferred_element_type=F32)


def _dot_tn(a, b):
    return lax.dot_general(a, b, (((0,), (0,)), ((), ())), preferred_element_type=F32)


def _split3(x):
    x_hi = x.astype(BF16).astype(F32)
    r1 = x - x_hi
    x_mid = r1.astype(BF16).astype(F32)
    x_lo = (r1 - x_mid).astype(BF16).astype(F32)
    return x_hi, x_mid, x_lo


def _split3_dot(m_bf16, x):
    x_hi, x_mid, x_lo = (t.astype(BF16) for t in _split3(x))
    return _dot(m_bf16, x_hi) + _dot(m_bf16, x_mid) + _dot(m_bf16, x_lo)


def _log_sigmoid(x):
    return jnp.minimum(x, 0.0) - jnp.log1p(jnp.exp(-jnp.abs(x)))


def _sigmoid(x):
    return 1.0 / (1.0 + jnp.exp(-x))


def _silu(x):
    return x * _sigmoid(x)


def _norm_modulate(x, g, shift, scale):
    ms = jnp.mean(x * x, axis=-1, keepdims=True)
    y = x * lax.rsqrt(ms + EPS) * g
    return y * (1.0 + scale) + shift


def _params(*sem):
    return pltpu.CompilerParams(dimension_semantics=sem, vmem_limit_bytes=VMEM_LIMIT)


def _layer_spec(arr, l):
    zeros = (0,) * (arr.ndim - 1)
    return pl.BlockSpec((1,) + arr.shape[1:], lambda *_: (l,) + zeros, pipeline_mode=pl.Buffered(1))


def _const_spec(arr):
    zeros = (0,) * arr.ndim
    return pl.BlockSpec(arr.shape, lambda *_: zeros, pipeline_mode=pl.Buffered(1))


def _ada_kernel(c_ref, w_ref, b_ref, o_ref):
    c_act = _silu(c_ref[...]).astype(BF16)
    o_ref[0] = _dot(c_act, w_ref[0].astype(BF16)) + b_ref[0]


def _ada_call(c, w_ada, b_ada):
    depth, d, n = w_ada.shape
    b = c.shape[0]
    tn = n // 4
    return pl.pallas_call(
        _ada_kernel,
        out_shape=jax.ShapeDtypeStruct((depth, b, n), F32),
        grid=(depth, n // tn),
        in_specs=[
            pl.BlockSpec((b, d), lambda l, j: (0, 0)),
            pl.BlockSpec((1, d, tn), lambda l, j: (l, 0, j)),
            pl.BlockSpec((1, 1, tn), lambda l, j: (l, 0, j)),
        ],
        out_specs=pl.BlockSpec((1, b, tn), lambda l, j: (l, 0, j)),
        compiler_params=_params("arbitrary", "arbitrary"),
        name="ada_mod",
    )(c, w_ada, b_ada.reshape(depth, 1, n))


def _win_prep_kernel(w_ref, main_ref, small_ref):
    tr = w_ref.shape[1]
    main_ref[0, :, 0:O_GLR] = w_ref[0, :, 0:O_GLR].astype(BF16)
    main_ref[0, :, O_GLR:W_MAIN] = w_ref[0, :, O_GG:O_FF].astype(BF16)
    small = jnp.concatenate(
        [w_ref[0, :, O_GLR:O_GG], w_ref[0, :, O_FF:O_FF + N_HEADS],
         jnp.zeros((tr, LANES - GLA_LOWRANK - N_HEADS), w_ref.dtype)], axis=1)
    small_ref[0] = small.astype(BF16)


def _win_prep_call(w_in):
    depth, d, n = w_in.shape
    tr = TR_PREP
    tile = lambda l, i: (l, i, 0)
    return pl.pallas_call(
        _win_prep_kernel,
        out_shape=(jax.ShapeDtypeStruct((depth, d, W_MAIN), BF16),
                   jax.ShapeDtypeStruct((depth, d, LANES), BF16)),
        grid=(depth, d // tr),
        in_specs=[pl.BlockSpec((1, tr, n), tile)],
        out_specs=(pl.BlockSpec((1, tr, W_MAIN), tile), pl.BlockSpec((1, tr, LANES), tile)),
        compiler_params=_params("arbitrary", "arbitrary"),
        name="win_prep",
    )(w_in)


def _inproj_kernel(x_ref, mod_ref, g1_ref, wmain_ref, wsmall_ref, wa2_ref, ba_ref, bfox_ref,
                   cos_ref, sin_ref, qg_ref, kg_ref, tri_ref,
                   r_ref, gl_ref, fq_ref, fk_ref, fv_ref, la_ref, carry_ref):
    d = x_ref.shape[2]
    w = BRANCH_W

    @pl.when(pl.program_id(1) == 0)
    def _():
        carry_ref[...] = jnp.zeros_like(carry_ref)

    x = x_ref[0]
    mod = mod_ref[0, 0]
    h = _norm_modulate(x, g1_ref[0], mod[:, 0:d], mod[:, d:2 * d])
    hb = h.astype(BF16)
    cos = cos_ref[...]
    sin = sin_ref[...]

    def heads_map(acc, fn):
        return jnp.concatenate(
            [fn(acc[:, i * HEAD_DIM:(i + 1) * HEAD_DIM]) for i in range(N_HEADS)], axis=1)

    def rope(t):
        return t * cos + pltpu.roll(t, HEAD_DIM // 2, axis=1) * sin

    def qk_norm(g):
        def fn(t):
            ms = jnp.mean(t * t, axis=-1, keepdims=True)
            return t * lax.rsqrt(ms + EPS) * g
        return fn

    def proj(c0, n):
        return _dot(hb, wmain_ref[0, :, c0:c0 + n])

    r_ref[0, :, 0:w] = heads_map(proj(0, w), rope).astype(BF16)
    r_ref[0, :, w:2 * w] = (heads_map(proj(w, w), rope) * HEAD_DIM ** -0.5).astype(BF16)
    r_ref[0, :, 2 * w:4 * w] = proj(2 * w, 2 * w).astype(BF16)
    c0 = 4 * w
    gqk = proj(c0, w)
    lane = lax.broadcasted_iota(jnp.int32, gqk.shape, 1)
    gqk = jnp.where(lane < _NK, gqk * GLA_DK ** -0.5, gqk)
    gl_ref[0, :, 0:w] = gqk.astype(BF16)
    gl_ref[0, :, w:3 * w] = proj(c0 + w, 2 * w).astype(BF16)
    small = _dot(hb, wsmall_ref[0])
    la_pre = _dot(small.astype(BF16), wa2_ref[0]) + ba_ref[0]
    la_ref[0] = _log_sigmoid(la_pre) * (1.0 / GLA_TAU)
    lane = lax.broadcasted_iota(jnp.int32, small.shape, 1)
    is_ff = (lane >= FF_LANE0) & (lane < FF_LANE0 + N_HEADS)
    log_f = jnp.where(is_ff, _log_sigmoid(small + bfox_ref[0]), 0.0)
    cum = _split3_dot(tri_ref[...], log_f) + carry_ref[...]
    carry_ref[...] = cum[cum.shape[0] - 1:cum.shape[0], :]

    c0 = 7 * w
    fq, fk, fv = proj(c0, w), proj(c0 + w, w), proj(c0 + 2 * w, w)
    q_norm, k_norm = qk_norm(qg_ref[0]), qk_norm(kg_ref[0])
    q_ones = jnp.where((lane >= 3) & (lane < AUG_ROWMAX0), 1.0, 0.0)
    k_ones = jnp.where((lane < 3) | ((lane >= AUG_ROWMAX0) & (lane < AUG_ROWMAX0 + 3)), 1.0, 0.0)
    for i in range(N_HEADS):
        sl = slice(i * HEAD_DIM, (i + 1) * HEAD_DIM)
        o = 2 * i * HEAD_DIM
        c = jnp.broadcast_to(cum[:, FF_LANE0 + i:FF_LANE0 + i + 1] * LOG2E, (cum.shape[0], LANES))
        c_hi, c_mid, c_lo = _split3(c)
        q_extra = jnp.where(lane == 0, c_hi, jnp.where(lane == 1, c_mid, jnp.where(lane == 2, c_lo, q_ones)))
        k_extra = jnp.where(lane == 3, -c_hi, jnp.where(lane == 4, -c_mid, jnp.where(lane == 5, -c_lo, k_ones)))
        fq_ref[0, :, o:o + HEAD_DIM] = (q_norm(fq[:, sl]) * (HEAD_DIM ** -0.5 * LOG2E)).astype(BF16)
        fq_ref[0, :, o + HEAD_DIM:o + 2 * HEAD_DIM] = q_extra.astype(BF16)
        fk_ref[0, :, o:o + HEAD_DIM] = k_norm(fk[:, sl]).astype(BF16)
        fk_ref[0, :, o + HEAD_DIM:o + 2 * HEAD_DIM] = k_extra.astype(BF16)
        fv_ref[0, :, o:o + HEAD_DIM] = fv[:, sl].astype(BF16)
        fv_ref[0, :, o + HEAD_DIM:o + 2 * HEAD_DIM] = jnp.ones((cum.shape[0], HEAD_DIM), BF16)


def _inproj_call(l, x, mod4, g1, wmain, wsmall, wa2, ba, bfox, cos_t, sin_t, qg, kg, tri):
    b, s, d = x.shape
    ts = min(TS_IN, s)
    w = BRANCH_W
    tile = lambda i, j: (i, j, 0)
    return pl.pallas_call(
        _inproj_kernel,
        out_shape=(
            jax.ShapeDtypeStruct((b, s, 4 * w), BF16),
            jax.ShapeDtypeStruct((b, s, 3 * w), BF16),
            jax.ShapeDtypeStruct((b, s, 2 * w), BF16),
            jax.ShapeDtypeStruct((b, s, 2 * w), BF16),
            jax.ShapeDtypeStruct((b, s, 2 * w), BF16),
            jax.ShapeDtypeStruct((b, s, _NK), F32),
        ),
        grid=(b, s // ts),
        in_specs=[
            pl.BlockSpec((1, ts, d), tile),
            pl.BlockSpec((1, 1, 1, mod4.shape[3]), lambda i, j: (l, i, 0, 0)),
            _layer_spec(g1, l),
            _layer_spec(wmain, l),
            _layer_spec(wsmall, l),
            _layer_spec(wa2, l),
            _layer_spec(ba, l),
            _layer_spec(bfox, l),
            pl.BlockSpec((ts, HEAD_DIM), lambda i, j: (j, 0)),
            pl.BlockSpec((ts, HEAD_DIM), lambda i, j: (j, 0)),
            _layer_spec(qg, l),
            _layer_spec(kg, l),
            _const_spec(tri),
        ],
        out_specs=(
            pl.BlockSpec((1, ts, 4 * w), tile),
            pl.BlockSpec((1, ts, 3 * w), tile),
            pl.BlockSpec((1, ts, 2 * w), tile),
            pl.BlockSpec((1, ts, 2 * w), tile),
            pl.BlockSpec((1, ts, 2 * w), tile),
            pl.BlockSpec((1, ts, _NK), tile),
        ),
        scratch_shapes=[pltpu.VMEM((1, LANES), F32)],
        compiler_params=_params("arbitrary", "arbitrary"),
        name="in_proj",
    )(x, mod4, g1, wmain, wsmall, wa2, ba, bfox, cos_t, sin_t, qg, kg, tri)


def _mixer_kernel(r_ref, gl_ref, la_ref, mret_ref, qw_ref, kw_ref, gts_ref, btri_ref, bones_ref,
                  rg_ref, gg_ref, ret_ref, gla_ref, rstate_ref, gstate_ref):
    w = BRANCH_W
    ts = r_ref.shape[1]

    @pl.when(pl.program_id(1) == 0)
    def _():
        rstate_ref[...] = jnp.zeros_like(rstate_ref)
        gstate_ref[...] = jnp.zeros_like(gstate_ref)

    for h in range(N_HEADS):
        sl = slice(h * HEAD_DIM, (h + 1) * HEAD_DIM)
        q = r_ref[0, :, sl]
        k = r_ref[0, :, w + h * HEAD_DIM:w + (h + 1) * HEAD_DIM]
        v = r_ref[0, :, 2 * w + h * HEAD_DIM:2 * w + (h + 1) * HEAD_DIM]
        gate = r_ref[0, :, 3 * w + h * HEAD_DIM:3 * w + (h + 1) * HEAD_DIM].astype(F32)
        state = rstate_ref[h]
        p = (_dot_nt(q, k) * mret_ref[h]).astype(BF16)
        o = _dot(p, v) + _dot(q, state.astype(BF16)) * qw_ref[h]
        kk = (k.astype(F32) * kw_ref[h]).astype(BF16)
        rstate_ref[h] = gts_ref[h] * state + _dot_tn(kk, v)
        mu = jnp.mean(o, axis=-1, keepdims=True)
        oc = o - mu
        var = jnp.mean(oc * oc, axis=-1, keepdims=True)
        on = oc * lax.rsqrt(var + EPS) * rg_ref[0, :, sl]
        ret_ref[0, :, sl] = (_silu(gate) * on).astype(BF16)

    la = la_ref[0]
    b_cum = _split3_dot(btri_ref[...], la)
    b_end = _split3_dot(bones_ref[...], la)
    kdec = jnp.exp(b_end - b_cum)
    a_all = jnp.exp(b_end)
    kk_all = (gl_ref[0, :, _NK:2 * _NK].astype(F32) * kdec).astype(BF16)
    q_all = gl_ref[0, :, 0:_NK]
    lane = lax.broadcasted_iota(jnp.int32, (ts, LANES), 1)
    chunks = [slice(c * CHUNK, (c + 1) * CHUNK) for c in range(ts // CHUNK)]
    pair_lanes = [slice((h // 2) * LANES, (h // 2 + 1) * LANES) for h in range(N_HEADS)]
    kvs = []
    for h in range(N_HEADS):
        kk = kk_all[:, pair_lanes[h]]
        v = gl_ref[0, :, w + h * HEAD_DIM:w + (h + 1) * HEAD_DIM]
        kvs.append([_dot_tn(v[rows], kk[rows]) for rows in chunks])
    states = []
    for h in range(N_HEADS):
        state = gstate_ref[h]
        per_chunk = []
        for c, rows in enumerate(chunks):
            state = a_all[c * CHUNK:c * CHUNK + 1, pair_lanes[h]] * state + kvs[h][c]
            per_chunk.append(state.astype(BF16))
        gstate_ref[h] = state
        states.append(per_chunk)
    for h in range(N_HEADS):
        own = (lane >= (h % 2) * GLA_DK) & (lane < (h % 2 + 1) * GLA_DK)
        qm = jnp.where(own, q_all[:, pair_lanes[h]], jnp.zeros((), BF16))
        gate = gl_ref[0, :, 2 * w + h * HEAD_DIM:2 * w + (h + 1) * HEAD_DIM].astype(F32)
        o = jnp.concatenate([_dot_nt(qm[rows], states[h][c]) for c, rows in enumerate(chunks)], axis=0)
        ms = jnp.mean(o * o, axis=-1, keepdims=True)
        on = o * lax.rsqrt(ms + EPS) * gg_ref[0]
        gla_ref[0, :, h * HEAD_DIM:(h + 1) * HEAD_DIM] = (_silu(gate) * on).astype(BF16)


def _mixer_call(l, r, gl, la, mret, qw, kw, gts, btri, bones, rg, gg):
    b, s, _ = r.shape
    ts = mret.shape[1]
    w = BRANCH_W
    tile = lambda i, j: (i, j, 0)
    return pl.pallas_call(
        _mixer_kernel,
        out_shape=(jax.ShapeDtypeStruct((b, s, w), BF16), jax.ShapeDtypeStruct((b, s, w), BF16)),
        grid=(b, s // ts),
        in_specs=[
            pl.BlockSpec((1, ts, 4 * w), tile),
            pl.BlockSpec((1, ts, 3 * w), tile),
            pl.BlockSpec((1, ts, _NK), tile),
            _const_spec(mret),
            _const_spec(qw),
            _const_spec(kw),
            _const_spec(gts),
            _const_spec(btri),
            _const_spec(bones),
            _layer_spec(rg, l),
            _layer_spec(gg, l),
        ],
        out_specs=(pl.BlockSpec((1, ts, w), tile), pl.BlockSpec((1, ts, w), tile)),
        scratch_shapes=[pltpu.VMEM((N_HEADS, HEAD_DIM, HEAD_DIM), F32),
                        pltpu.VMEM((N_HEADS, HEAD_DIM, LANES), F32)],
        compiler_params=_params("arbitrary", "arbitrary"),
        name="ret_gla_mixer",
    )(r, gl, la, mret, qw, kw, gts, btri, bones, rg, gg)


def _fox_kernel(q_ref, k_ref, v_ref, o_ref, mpart_ref, q2_ref, acc_ref):
    tq = q_ref.shape[1]
    qi = pl.program_id(1)
    hw = 2 * HEAD_DIM
    hs = [slice(h * hw, (h + 1) * hw) for h in range(N_HEADS)]
    row = lax.broadcasted_iota(jnp.int32, (tq, tq), 0)
    col = lax.broadcasted_iota(jnp.int32, (tq, tq), 1)
    diagonal = row >= col

    def scores(lhs, rows, mask):
        out = [_dot_nt(lhs(h), k_ref[0, rows, hs[h]]) for h in range(N_HEADS)]
        if mask is not None:
            out = [jnp.where(mask, s, -jnp.inf) for s in out]
        return out

    def pass1(start, width, mask):
        rows = pl.ds(pl.multiple_of(start, width), width)
        for h, s in enumerate(scores(lambda h: q_ref[0, :, hs[h]], rows, mask)):
            m = mpart_ref[h]
            for t in range(width // LANES):
                m = jnp.maximum(m, s[:, t * LANES:(t + 1) * LANES])
            mpart_ref[h] = m

    def pass2(start, width, mask):
        rows = pl.ds(pl.multiple_of(start, width), width)
        ps = [jnp.exp2(s).astype(BF16) for s in scores(lambda h: q2_ref[h], rows, mask)]
        for h in range(N_HEADS):
            acc_ref[h] += _dot(ps[h], v_ref[0, rows, hs[h]])

    def loop(fn):
        def body(jb, carry):
            fn(jb * tq, tq, None)
            return carry
        lax.fori_loop(0, qi, body, 0)
        fn(qi * tq, tq, diagonal)

    mpart_ref[...] = jnp.full(mpart_ref.shape, -jnp.inf, F32)
    loop(pass1)
    lane = lax.broadcasted_iota(jnp.int32, (tq, LANES), 1)
    for h in range(N_HEADS):
        m = jnp.broadcast_to(jnp.max(mpart_ref[h], axis=-1, keepdims=True), (tq, LANES))
        m_hi, m_mid, m_lo = _split3(m)
        extra = q_ref[0, :, h * hw + HEAD_DIM:(h + 1) * hw].astype(F32)
        extra = jnp.where(lane == AUG_ROWMAX0, -m_hi, jnp.where(
            lane == AUG_ROWMAX0 + 1, -m_mid, jnp.where(lane == AUG_ROWMAX0 + 2, -m_lo, extra)))
        q2_ref[h, :, 0:HEAD_DIM] = q_ref[0, :, h * hw:h * hw + HEAD_DIM]
        q2_ref[h, :, HEAD_DIM:hw] = extra.astype(BF16)
    acc_ref[...] = jnp.zeros_like(acc_ref)
    loop(pass2)
    for h in range(N_HEADS):
        acc = acc_ref[h]
        o_ref[0, :, h * HEAD_DIM:(h + 1) * HEAD_DIM] = (acc[:, 0:HEAD_DIM] / acc[:, HEAD_DIM:hw]).astype(BF16)


def _fox_call(fq, fk, fv, tq):
    b, s, wide = fq.shape
    assert s % tq == 0
    return pl.pallas_call(
        _fox_kernel,
        out_shape=jax.ShapeDtypeStruct((b, s, BRANCH_W), BF16),
        grid=(b, s // tq),
        in_specs=[
            pl.BlockSpec((1, tq, wide), lambda i, j: (i, j, 0)),
            pl.BlockSpec((1, s, wide), lambda i, j: (i, 0, 0)),
            pl.BlockSpec((1, s, wide), lambda i, j: (i, 0, 0)),
        ],
        out_specs=pl.BlockSpec((1, tq, BRANCH_W), lambda i, j: (i, j, 0)),
        scratch_shapes=[pltpu.VMEM((N_HEADS, tq, LANES), F32),
                        pltpu.VMEM((N_HEADS, tq, 2 * HEAD_DIM), BF16),
                        pltpu.VMEM((N_HEADS, tq, 2 * HEAD_DIM), F32)],
        compiler_params=_params("arbitrary", "arbitrary"),
        name="fox_attention",
    )(fq, fk, fv)


def _outproj_kernel(x_ref, mod_ref, g1_ref, ret_ref, gla_ref, fox_ref, wbr_ref, wmg_ref, bmg_ref, wo_ref,
                    o_ref):
    d = x_ref.shape[2]
    x = x_ref[0]
    mod = mod_ref[0, 0]
    hb = _norm_modulate(x, g1_ref[0], mod[:, 0:d], mod[:, d:2 * d]).astype(BF16)
    mixed = None
    for n, br_ref in enumerate((ret_ref, gla_ref, fox_ref)):
        y = _dot(br_ref[0], wbr_ref[0, n])
        gate = _sigmoid(_dot(hb, wmg_ref[0, :, n * d:(n + 1) * d]) + bmg_ref[0, :, n * d:(n + 1) * d])
        mixed = gate * y if mixed is None else mixed + gate * y
    out = _dot(mixed.astype(BF16), wo_ref[0])
    o_ref[0] = x + mod[:, 2 * d:3 * d] * out


def _outproj_call(l, x, mod4, g1, ret, gla, fox, wbr, wmg, bmg, wo):
    b, s, d = x.shape
    ts = min(TS_OUT, s)
    tile = lambda i, j: (i, j, 0)
    return pl.pallas_call(
        _outproj_kernel,
        out_shape=jax.ShapeDtypeStruct((b, s, d), F32),
        grid=(b, s // ts),
        in_specs=[
            pl.BlockSpec((1, ts, d), tile),
            pl.BlockSpec((1, 1, 1, mod4.shape[3]), lambda i, j: (l, i, 0, 0)),
            _layer_spec(g1, l),
            pl.BlockSpec((1, ts, BRANCH_W), tile),
            pl.BlockSpec((1, ts, BRANCH_W), tile),
            pl.BlockSpec((1, ts, BRANCH_W), tile),
            _layer_spec(wbr, l),
            _layer_spec(wmg, l),
            _layer_spec(bmg, l),
            _layer_spec(wo, l),
        ],
        out_specs=pl.BlockSpec((1, ts, d), tile),
        compiler_params=_params("arbitrary", "arbitrary"),
        name="out_proj",
    )(x, mod4, g1, ret, gla, fox, wbr, wmg, bmg, wo)


def _ffn_kernel(x_ref, mod_ref, g2_ref, wup_ref, wconv_ref, bconv_ref, wdown_ref, o_ref, tail_ref):
    d = x_ref.shape[2]
    ts = x_ref.shape[1]
    ff = wdown_ref.shape[1]

    @pl.when(pl.program_id(1) == 0)
    def _():
        tail_ref[...] = jnp.zeros_like(tail_ref)

    x = x_ref[0]
    mod = mod_ref[0, 0]
    hb = _norm_modulate(x, g2_ref[0], mod[:, 3 * d:4 * d], mod[:, 4 * d:5 * d]).astype(BF16)
    u = _dot(hb, wup_ref[0, :, 0:ff])
    g = _dot(hb, wup_ref[0, :, ff:2 * ff])
    row = lax.broadcasted_iota(jnp.int32, u.shape, 0)
    prev1 = tail_ref[CONV_W - 2:CONV_W - 1, :]
    prev2 = tail_ref[CONV_W - 3:CONV_W - 2, :]
    u1 = jnp.where(row == 0, prev1, pltpu.roll(u, 1, axis=0))
    u2 = jnp.where(row == 0, prev2, jnp.where(row == 1, prev1, pltpu.roll(u, 2, axis=0)))
    tail_ref[...] = u[ts - (CONV_W - 1):ts, :]
    wc = wconv_ref[0]
    conv = bconv_ref[0] + wc[0:1, :] * u2 + wc[1:2, :] * u1 + wc[2:3, :] * u
    act = (_silu(conv) * g).astype(BF16)
    y = _dot(act, wdown_ref[0])
    o_ref[0] = x + mod[:, 5 * d:6 * d] * y


def _ffn_call(l, x, mod4, g2, wup, wconv, bconv, wdown):
    b, s, d = x.shape
    ts = min(TS_FFN, s)
    ff = wdown.shape[1]
    tile = lambda i, j: (i, j, 0)
    return pl.pallas_call(
        _ffn_kernel,
        out_shape=jax.ShapeDtypeStruct((b, s, d), F32),
        grid=(b, s // ts),
        in_specs=[
            pl.BlockSpec((1, ts, d), tile),
            pl.BlockSpec((1, 1, 1, mod4.shape[3]), lambda i, j: (l, i, 0, 0)),
            _layer_spec(g2, l),
            _layer_spec(wup, l),
            _layer_spec(wconv, l),
            _layer_spec(bconv, l),
            _layer_spec(wdown, l),
        ],
        out_specs=pl.BlockSpec((1, ts, d), tile),
        scratch_shapes=[pltpu.VMEM((CONV_W - 1, ff), F32)],
        compiler_params=_params("arbitrary", "arbitrary"),
        name="conv_ffn",
    )(x, mod4, g2, wup, wconv, bconv, wdown)


def _rope_tables(s):
    half = HEAD_DIM // 2
    pos = jnp.arange(s, dtype=F32)
    inv_freq = ROPE_BASE ** (-jnp.arange(half, dtype=F32) / half)
    ang = pos[:, None] * inv_freq[None, :]
    cos, sin = jnp.cos(ang), jnp.sin(ang)
    return jnp.concatenate([cos, cos], axis=1), jnp.concatenate([-sin, sin], axis=1)


def _retention_tables(ts):
    log_g = jnp.log1p(-jnp.exp2(-5.0 - jnp.arange(N_HEADS, dtype=F32)))[:, None, None]
    t = jnp.arange(ts, dtype=F32)
    diff = t[:, None] - t[None, :]
    same = (jnp.arange(ts)[:, None] // CHUNK) == (jnp.arange(ts)[None, :] // CHUNK)
    expo = jnp.where(same, jnp.abs(diff), diff)[None]
    mask = jnp.where((same | (diff > 0))[None], jnp.exp(jnp.where(expo >= 0, expo, 0.0) * log_g), 0.0)
    ones = jnp.ones((1, 1, HEAD_DIM), F32)
    qw = jnp.exp((t + 1.0)[None, :, None] * log_g) * ones
    kw = jnp.exp((ts - 1.0 - t)[None, :, None] * log_g) * ones
    gts = jnp.exp(ts * log_g) * ones
    return mask, qw, kw, gts


def _chunk_matrices(ts):
    i = jnp.arange(ts)
    same = (i[:, None] // CHUNK) == (i[None, :] // CHUNK)
    btri = (same & (i[:, None] >= i[None, :])).astype(BF16)
    return btri, same.astype(BF16)


def kernel(x, c, norm1_g, norm2_g, w_ada, b_ada, w_in, w_gla_a2, b_gla_a, b_fox_f, ret_norm_g, gla_norm_g,
           q_norm_g, k_norm_g, w_br, w_mg, b_mg, w_o, w_up, w_conv, b_conv, w_down):
    b, s, d = x.shape
    depth = w_ada.shape[0]
    row = lambda a: a[:, None, :]

    mod4 = _ada_call(c, w_ada, b_ada)[:, :, None, :]
    wmain, wsmall = _win_prep_call(w_in.astype(BF16))
    wa2 = jnp.pad(w_gla_a2, ((0, 0), (0, LANES - GLA_LOWRANK), (0, 0))).astype(BF16)
    bfox = jnp.pad(b_fox_f, ((0, 0), (FF_LANE0, LANES - FF_LANE0 - N_HEADS)))
    wbr, wmg, wo, wup, wdown = (a.astype(BF16) for a in (w_br, w_mg, w_o, w_up, w_down))
    g1, g2, ba, bfox, rg, gg, qg, kg, bmg, bconv = (
        row(a) for a in (norm1_g, norm2_g, b_gla_a, bfox, ret_norm_g, gla_norm_g, q_norm_g, k_norm_g,
                         b_mg, b_conv))

    ts_in = min(TS_IN, s)
    ts_mix = min(TS_MIX, s)
    i = jnp.arange(ts_in)
    tri = (i[:, None] >= i[None, :]).astype(BF16)
    cos_t, sin_t = _rope_tables(s)
    mret, qw, kw, gts = _retention_tables(ts_mix)
    btri, bones = _chunk_matrices(ts_mix)

    for l in range(depth):
        r, gl, fq, fk, fv, la = _inproj_call(
            l, x, mod4, g1, wmain, wsmall, wa2, ba, bfox, cos_t, sin_t, qg, kg, tri)
        ret, gla_o = _mixer_call(l, r, gl, la, mret, qw, kw, gts, btri, bones, rg, gg)
        fox = _fox_call(fq, fk, fv, min(TQ_FOX, s))
        x = _outproj_call(l, x, mod4, g1, ret, gla_o, fox, wbr, wmg, bmg, wo)
        x = _ffn_call(l, x, mod4, g2, wup, w_conv, bconv, wdown)
    return x
```

```python
import jax
import jax.numpy as jnp
from jax import lax
from jax.experimental import pallas as pl
from jax.experimental.pallas import tpu as pltpu

F32 = jnp.float32
BF16 = jnp.bfloat16

CHUNK = 64
N_HEADS = 4
HEAD_DIM = 128
BRANCH_W = N_HEADS * HEAD_DIM
GLA_DK = HEAD_DIM // 2
GLA_LOWRANK = 16
GLA_TAU = 16.0
N_BRANCH = 3
CONV_W = 3
ROPE_BASE = 10000.0
EPS = 1e-6

LANES = 128
SUBLANES = 8
FF_LANE0 = GLA_LOWRANK
AUG_ROWMAX0 = 6
LOG2E = 1.4426950408889634

VMEM_LIMIT = 56 * 1024 * 1024

TS_IN = 512
TS_MIX = 256
TQ_FOX = 512
TS_OUT = 512
TS_FFN = 512
TR_PREP = 256
_NK = N_HEADS * GLA_DK
O_GLR = 4 * BRANCH_W + 2 * _NK + BRANCH_W
O_GG = O_GLR + GLA_LOWRANK
O_FF = O_GG + 4 * BRANCH_W
W_MAIN = O_GLR + (O_FF - O_GG)


def _dot(a, b):
    return jnp.dot(a, b, preferred_element_type=F32)


def _dot_nt(a, b):
    return lax.dot_general(a, b, (((1,), (1,)), ((), ())), preferred_element_type=F32)


def _dot_tn(a, b):
    return lax.dot_general(a, b, (((0,), (0,)), ((), ())), preferred_element_type=F32)


def _split3(x):
    x_hi = x.astype(BF16).astype(F32)
    r1 = x - x_hi
    x_mid = r1.astype(BF16).astype(F32)
    x_lo = (r1 - x_mid).astype(BF16).astype(F32)
    return x_hi, x_mid, x_lo


def _split3_dot(m_bf16, x):
    x_hi, x_mid, x_lo = (t.astype(BF16) for t in _split3(x))
    return _dot(m_bf16, x_hi) + _dot(m_bf16, x_mid) + _dot(m_bf16, x_lo)


def _log_sigmoid(x):
    return jnp.minimum(x, 0.0) - jnp.log1p(jnp.exp(-jnp.abs(x)))


def _sigmoid(x):
    return 1.0 / (1.0 + jnp.exp(-x))


def _silu(x):
    return x * _sigmoid(x)


def _norm_modulate(x, g, shift, scale):
    ms = jnp.mean(x * x, axis=-1, keepdims=True)
    y = x * lax.rsqrt(ms + EPS) * g
    return y * (1.0 + scale) + shift


def _params(*sem):
    return pltpu.CompilerParams(dimension_semantics=sem, vmem_limit_bytes=VMEM_LIMIT)


def _layer_spec(arr, l):
    zeros = (0,) * (arr.ndim - 1)
    return pl.BlockSpec((1,) + arr.shape[1:], lambda *_: (l,) + zeros, pipeline_mode=pl.Buffered(1))


def _const_spec(arr):
    zeros = (0,) * arr.ndim
    return pl.BlockSpec(arr.shape, lambda *_: zeros, pipeline_mode=pl.Buffered(1))


def _ada_kernel(c_ref, w_ref, b_ref, o_ref):
    c_act = _silu(c_ref[...]).astype(BF16)
    o_ref[0] = _dot(c_act, w_ref[0].astype(BF16)) + b_ref[0]


def _ada_call(c, w_ada, b_ada):
    depth, d, n = w_ada.shape
    b = c.shape[0]
    tn = n // 4
    return pl.pallas_call(
        _ada_kernel,
        out_shape=jax.ShapeDtypeStruct((depth, b, n), F32),
        grid=(depth, n // tn),
        in_specs=[
            pl.BlockSpec((b, d), lambda l, j: (0, 0)),
            pl.BlockSpec((1, d, tn), lambda l, j: (l, 0, j)),
            pl.BlockSpec((1, 1, tn), lambda l, j: (l, 0, j)),
        ],
        out_specs=pl.BlockSpec((1, b, tn), lambda l, j: (l, 0, j)),
        compiler_params=_params("arbitrary", "arbitrary"),
        name="ada_mod",
    )(c, w_ada, b_ada.reshape(depth, 1, n))


def _win_prep_kernel(w_ref, main_ref, small_ref):
    tr = w_ref.shape[1]
    main_ref[0, :, 0:O_GLR] = w_ref[0, :, 0:O_GLR].astype(BF16)
    main_ref[0, :, O_GLR:W_MAIN] = w_ref[0, :, O_GG:O_FF].astype(BF16)
    small = jnp.concatenate(
        [w_ref[0, :, O_GLR:O_GG], w_ref[0, :, O_FF:O_FF + N_HEADS],
         jnp.zeros((tr, LANES - GLA_LOWRANK - N_HEADS), w_ref.dtype)], axis=1)
    small_ref[0] = small.astype(BF16)


def _win_prep_call(w_in):
    depth, d, n = w_in.shape
    tr = TR_PREP
    tile = lambda l, i: (l, i, 0)
    return pl.pallas_call(
        _win_prep_kernel,
        out_shape=(jax.ShapeDtypeStruct((depth, d, W_MAIN), BF16),
                   jax.ShapeDtypeStruct((depth, d, LANES), BF16)),
        grid=(depth, d // tr),
        in_specs=[pl.BlockSpec((1, tr, n), tile)],
        out_specs=(pl.BlockSpec((1, tr, W_MAIN), tile), pl.BlockSpec((1, tr, LANES), tile)),
        compiler_params=_params("arbitrary", "arbitrary"),
        name="win_prep",
    )(w_in)


def _inproj_kernel(x_ref, mod_ref, g1_ref, wmain_ref, wsmall_ref, wa2_ref, ba_ref, bfox_ref,
                   cos_ref, sin_ref, qg_ref, kg_ref, tri_ref,
                   r_ref, gl_ref, fq_ref, fk_ref, fv_ref, la_ref, carry_ref):
    d = x_ref.shape[2]
    w = BRANCH_W

    @pl.when(pl.program_id(1) == 0)
    def _():
        carry_ref[...] = jnp.zeros_like(carry_ref)

    x = x_ref[0]
    mod = mod_ref[0, 0]
    h = _norm_modulate(x, g1_ref[0], mod[:, 0:d], mod[:, d:2 * d])
    hb = h.astype(BF16)
    cos = cos_ref[...]
    sin = sin_ref[...]

    def heads_map(acc, fn):
        return jnp.concatenate(
            [fn(acc[:, i * HEAD_DIM:(i + 1) * HEAD_DIM]) for i in range(N_HEADS)], axis=1)

    def rope(t):
        return t * cos + pltpu.roll(t, HEAD_DIM // 2, axis=1) * sin

    def qk_norm(g):
        def fn(t):
            ms = jnp.mean(t * t, axis=-1, keepdims=True)
            return t * lax.rsqrt(ms + EPS) * g
        return fn

    def proj(c0, n):
        return _dot(hb, wmain_ref[0, :, c0:c0 + n])

    small = _dot(hb, wsmall_ref[0])
    c0 = 7 * w
    fq, fk = proj(c0, w), proj(c0 + w, w)
    la_pre = _dot(small.astype(BF16), wa2_ref[0]) + ba_ref[0]
    la_ref[0] = _log_sigmoid(la_pre) * (1.0 / GLA_TAU)
    lane = lax.broadcasted_iota(jnp.int32, small.shape, 1)
    is_ff = (lane >= FF_LANE0) & (lane < FF_LANE0 + N_HEADS)
    log_f = jnp.where(is_ff, _log_sigmoid(small + bfox_ref[0]), 0.0)
    cum = _split3_dot(tri_ref[...], log_f) + carry_ref[...]
    carry_ref[...] = cum[cum.shape[0] - 1:cum.shape[0], :]
    rq, rk = proj(0, w), proj(w, w)

    q_norm, k_norm = qk_norm(qg_ref[0]), qk_norm(kg_ref[0])
    q_ones = jnp.where((lane >= 3) & (lane < AUG_ROWMAX0), 1.0, 0.0)
    k_ones = jnp.where((lane < 3) | ((lane >= AUG_ROWMAX0) & (lane < AUG_ROWMAX0 + 3)), 1.0, 0.0)
    for i in range(N_HEADS):
        sl = slice(i * HEAD_DIM, (i + 1) * HEAD_DIM)
        o = 2 * i * HEAD_DIM
        c = jnp.broadcast_to(cum[:, FF_LANE0 + i:FF_LANE0 + i + 1] * LOG2E, (cum.shape[0], LANES))
        c_hi, c_mid, c_lo = _split3(c)
        q_extra = jnp.where(lane == 0, c_hi, jnp.where(lane == 1, c_mid, jnp.where(lane == 2, c_lo, q_ones)))
        k_extra = jnp.where(lane == 3, -c_hi, jnp.where(lane == 4, -c_mid, jnp.where(lane == 5, -c_lo, k_ones)))
        fq_ref[0, :, o:o + HEAD_DIM] = (q_norm(fq[:, sl]) * (HEAD_DIM ** -0.5 * LOG2E)).astype(BF16)
        fq_ref[0, :, o + HEAD_DIM:o + 2 * HEAD_DIM] = q_extra.astype(BF16)
        fk_ref[0, :, o:o + HEAD_DIM] = k_norm(fk[:, sl]).astype(BF16)
        fk_ref[0, :, o + HEAD_DIM:o + 2 * HEAD_DIM] = k_extra.astype(BF16)

    gqk = proj(4 * w, w)
    r_ref[0, :, 0:w] = heads_map(rq, rope).astype(BF16)
    r_ref[0, :, w:2 * w] = (heads_map(rk, rope) * HEAD_DIM ** -0.5).astype(BF16)
    gqk = jnp.where(lax.broadcasted_iota(jnp.int32, gqk.shape, 1) < _NK, gqk * GLA_DK ** -0.5, gqk)
    gl_ref[0, :, 0:w] = gqk.astype(BF16)
    r_ref[0, :, 3 * w:4 * w] = _silu(proj(3 * w, w)).astype(BF16)
    gl_ref[0, :, 2 * w:3 * w] = _silu(proj(6 * w, w)).astype(BF16)
    r_ref[0, :, 2 * w:3 * w] = proj(2 * w, w).astype(BF16)
    gl_ref[0, :, w:2 * w] = proj(5 * w, w).astype(BF16)
    fv = proj(c0 + 2 * w, w)
    for i in range(N_HEADS):
        o = 2 * i * HEAD_DIM
        fv_ref[0, :, o:o + HEAD_DIM] = fv[:, i * HEAD_DIM:(i + 1) * HEAD_DIM].astype(BF16)
        fv_ref[0, :, o + HEAD_DIM:o + 2 * HEAD_DIM] = jnp.ones((fv.shape[0], HEAD_DIM), BF16)


def _inproj_call(l, x, mod4, g1, wmain, wsmall, wa2, ba, bfox, cos_t, sin_t, qg, kg, tri):
    b, s, d = x.shape
    ts = min(TS_IN, s)
    w = BRANCH_W
    tile = lambda i, j: (i, j, 0)
    return pl.pallas_call(
        _inproj_kernel,
        out_shape=(
            jax.ShapeDtypeStruct((b, s, 4 * w), BF16),
            jax.ShapeDtypeStruct((b, s, 3 * w), BF16),
            jax.ShapeDtypeStruct((b, s, 2 * w), BF16),
            jax.ShapeDtypeStruct((b, s, 2 * w), BF16),
            jax.ShapeDtypeStruct((b, s, 2 * w), BF16),
            jax.ShapeDtypeStruct((b, s, _NK), F32),
        ),
        grid=(b, s // ts),
        in_specs=[
            pl.BlockSpec((1, ts, d), tile),
            pl.BlockSpec((1, 1, 1, mod4.shape[3]), lambda i, j: (l, i, 0, 0)),
            _layer_spec(g1, l),
            _layer_spec(wmain, l),
            _layer_spec(wsmall, l),
            _layer_spec(wa2, l),
            _layer_spec(ba, l),
            _layer_spec(bfox, l),
            pl.BlockSpec((ts, HEAD_DIM), lambda i, j: (j, 0)),
            pl.BlockSpec((ts, HEAD_DIM), lambda i, j: (j, 0)),
            _layer_spec(qg, l),
            _layer_spec(kg, l),
            _const_spec(tri),
        ],
        out_specs=(
            pl.BlockSpec((1, ts, 4 * w), tile),
            pl.BlockSpec((1, ts, 3 * w), tile),
            pl.BlockSpec((1, ts, 2 * w), tile),
            pl.BlockSpec((1, ts, 2 * w), tile),
            pl.BlockSpec((1, ts, 2 * w), tile),
            pl.BlockSpec((1, ts, _NK), tile),
        ),
        scratch_shapes=[pltpu.VMEM((1, LANES), F32)],
        compiler_params=_params("arbitrary", "arbitrary"),
        name="in_proj",
    )(x, mod4, g1, wmain, wsmall, wa2, ba, bfox, cos_t, sin_t, qg, kg, tri)


def _mixer_kernel(r_ref, gl_ref, la_ref, mret_ref, qw_ref, kw_ref, gts_ref, btri_ref, bones_ref,
                  rg_ref, gg_ref, ret_ref, gla_ref, rstate_ref, gstate_ref):
    w = BRANCH_W
    ts = r_ref.shape[1]

    @pl.when(pl.program_id(1) == 0)
    def _():
        rstate_ref[...] = jnp.zeros_like(rstate_ref)
        gstate_ref[...] = jnp.zeros_like(gstate_ref)

    for h in range(N_HEADS):
        sl = slice(h * HEAD_DIM, (h + 1) * HEAD_DIM)
        q = r_ref[0, :, sl]
        k = r_ref[0, :, w + h * HEAD_DIM:w + (h + 1) * HEAD_DIM]
        v = r_ref[0, :, 2 * w + h * HEAD_DIM:2 * w + (h + 1) * HEAD_DIM]
        gate = r_ref[0, :, 3 * w + h * HEAD_DIM:3 * w + (h + 1) * HEAD_DIM].astype(F32)
        state = rstate_ref[h]
        p = (_dot_nt(q, k) * mret_ref[h]).astype(BF16)
        o = _dot(p, v) + _dot(q, state.astype(BF16)) * qw_ref[h]
        kk = (k.astype(F32) * kw_ref[h]).astype(BF16)
        rstate_ref[h] = gts_ref[h] * state + _dot_tn(kk, v)
        mu = jnp.mean(o, axis=-1, keepdims=True)
        oc = o - mu
        var = jnp.mean(oc * oc, axis=-1, keepdims=True)
        on = oc * lax.rsqrt(var + EPS) * rg_ref[0, :, sl]
        ret_ref[0, :, sl] = (gate * on).astype(BF16)

    la = la_ref[0]
    b_cum = _split3_dot(btri_ref[...], la)
    b_end = _split3_dot(bones_ref[...], la)
    kdec = jnp.exp(b_end - b_cum)
    a_all = jnp.exp(b_end)
    kk_all = (gl_ref[0, :, _NK:2 * _NK].astype(F32) * kdec).astype(BF16)
    q_all = gl_ref[0, :, 0:_NK]
    lane = lax.broadcasted_iota(jnp.int32, (ts, LANES), 1)
    chunks = [slice(c * CHUNK, (c + 1) * CHUNK) for c in range(ts // CHUNK)]
    pair_lanes = [slice((h // 2) * LANES, (h // 2 + 1) * LANES) for h in range(N_HEADS)]
    kvs = []
    for h in range(N_HEADS):
        kk = kk_all[:, pair_lanes[h]]
        v = gl_ref[0, :, w + h * HEAD_DIM:w + (h + 1) * HEAD_DIM]
        kvs.append([_dot_tn(v[rows], kk[rows]) for rows in chunks])
    states = []
    for h in range(N_HEADS):
        state = gstate_ref[h]
        per_chunk = []
        for c, rows in enumerate(chunks):
            state = a_all[c * CHUNK:c * CHUNK + 1, pair_lanes[h]] * state + kvs[h][c]
            per_chunk.append(state.astype(BF16))
        gstate_ref[h] = state
        states.append(per_chunk)
    for h in range(N_HEADS):
        own = (lane >= (h % 2) * GLA_DK) & (lane < (h % 2 + 1) * GLA_DK)
        qm = jnp.where(own, q_all[:, pair_lanes[h]], jnp.zeros((), BF16))
        gate = gl_ref[0, :, 2 * w + h * HEAD_DIM:2 * w + (h + 1) * HEAD_DIM].astype(F32)
        o = jnp.concatenate([_dot_nt(qm[rows], states[h][c]) for c, rows in enumerate(chunks)], axis=0)
        ms = jnp.mean(o * o, axis=-1, keepdims=True)
        on = o * lax.rsqrt(ms + EPS) * gg_ref[0]
        gla_ref[0, :, h * HEAD_DIM:(h + 1) * HEAD_DIM] = (gate * on).astype(BF16)


def _mixer_call(l, r, gl, la, mret, qw, kw, gts, btri, bones, rg, gg):
    b, s, _ = r.shape
    ts = mret.shape[1]
    w = BRANCH_W
    tile = lambda i, j: (i, j, 0)
    return pl.pallas_call(
        _mixer_kernel,
        out_shape=(jax.ShapeDtypeStruct((b, s, w), BF16), jax.ShapeDtypeStruct((b, s, w), BF16)),
        grid=(b, s // ts),
        in_specs=[
            pl.BlockSpec((1, ts, 4 * w), tile),
            pl.BlockSpec((1, ts, 3 * w), tile),
            pl.BlockSpec((1, ts, _NK), tile),
            _const_spec(mret),
            _const_spec(qw),
            _const_spec(kw),
            _const_spec(gts),
            _const_spec(btri),
            _const_spec(bones),
            _layer_spec(rg, l),
            _layer_spec(gg, l),
        ],
        out_specs=(pl.BlockSpec((1, ts, w), tile), pl.BlockSpec((1, ts, w), tile)),
        scratch_shapes=[pltpu.VMEM((N_HEADS, HEAD_DIM, HEAD_DIM), F32),
                        pltpu.VMEM((N_HEADS, HEAD_DIM, LANES), F32)],
        compiler_params=_params("arbitrary", "arbitrary"),
        name="ret_gla_mixer",
    )(r, gl, la, mret, qw, kw, gts, btri, bones, rg, gg)


def _fox_kernel(q_ref, k_ref, v_ref, o_ref, mpart_ref, q2_ref, acc_ref):
    tq = q_ref.shape[1]
    qi = pl.program_id(1)
    hw = 2 * HEAD_DIM
    hs = [slice(h * hw, (h + 1) * hw) for h in range(N_HEADS)]
    half = tq // 2
    whole, top, bottom = slice(0, tq), slice(0, half), slice(half, tq)
    top_mask = (lax.broadcasted_iota(jnp.int32, (half, half), 0)
                >= lax.broadcasted_iota(jnp.int32, (half, half), 1))
    bottom_mask = (lax.broadcasted_iota(jnp.int32, (half, tq), 1)
                   <= lax.broadcasted_iota(jnp.int32, (half, tq), 0) + half)

    def scores(lhs, keys, mask):
        out = [_dot_nt(lhs(h), k_ref[0, keys, hs[h]]) for h in range(N_HEADS)]
        if mask is not None:
            out = [jnp.where(mask, s, -jnp.inf) for s in out]
        return out

    def pass1(qrows, start, width, mask):
        keys = pl.ds(pl.multiple_of(start, width), width)
        for h, s in enumerate(scores(lambda h: q_ref[0, qrows, hs[h]], keys, mask)):
            m = mpart_ref[h, qrows]
            for t in range(width // LANES):
                m = jnp.maximum(m, s[:, t * LANES:(t + 1) * LANES])
            mpart_ref[h, qrows] = m

    def pass2(qrows, start, width, mask):
        keys = pl.ds(pl.multiple_of(start, width), width)
        ps = [jnp.exp2(s).astype(BF16) for s in scores(lambda h: q2_ref[h, qrows], keys, mask)]
        for h in range(N_HEADS):
            acc_ref[h, qrows] += _dot(ps[h], v_ref[0, keys, hs[h]])

    def loop(fn):
        def body(jb, carry):
            fn(whole, jb * tq, tq, None)
            return carry
        lax.fori_loop(0, qi, body, 0)
        fn(top, qi * tq, half, top_mask)
        fn(bottom, qi * tq, tq, bottom_mask)

    mpart_ref[...] = jnp.full(mpart_ref.shape, -jnp.inf, F32)
    loop(pass1)
    lane = lax.broadcasted_iota(jnp.int32, (tq, LANES), 1)
    for h in range(N_HEADS):
        m = jnp.broadcast_to(jnp.max(mpart_ref[h], axis=-1, keepdims=True), (tq, LANES))
        m_hi, m_mid, m_lo = _split3(m)
        extra = q_ref[0, :, h * hw + HEAD_DIM:(h + 1) * hw].astype(F32)
        extra = jnp.where(lane == AUG_ROWMAX0, -m_hi, jnp.where(
            lane == AUG_ROWMAX0 + 1, -m_mid, jnp.where(lane == AUG_ROWMAX0 + 2, -m_lo, extra)))
        q2_ref[h, :, 0:HEAD_DIM] = q_ref[0, :, h * hw:h * hw + HEAD_DIM]
        q2_ref[h, :, HEAD_DIM:hw] = extra.astype(BF16)
    acc_ref[...] = jnp.zeros_like(acc_ref)
    loop(pass2)
    for h in range(N_HEADS):
        acc = acc_ref[h]
        o_ref[0, :, h * HEAD_DIM:(h + 1) * HEAD_DIM] = (acc[:, 0:HEAD_DIM] / acc[:, HEAD_DIM:hw]).astype(BF16)


def _fox_call(fq, fk, fv, tq):
    b, s, wide = fq.shape
    assert s % tq == 0
    return pl.pallas_call(
        _fox_kernel,
        out_shape=jax.ShapeDtypeStruct((b, s, BRANCH_W), BF16),
        grid=(b, s // tq),
        in_specs=[
            pl.BlockSpec((1, tq, wide), lambda i, j: (i, j, 0)),
            pl.BlockSpec((1, s, wide), lambda i, j: (i, 0, 0)),
            pl.BlockSpec((1, s, wide), lambda i, j: (i, 0, 0)),
        ],
        out_specs=pl.BlockSpec((1, tq, BRANCH_W), lambda i, j: (i, j, 0)),
        scratch_shapes=[pltpu.VMEM((N_HEADS, tq, LANES), F32),
                        pltpu.VMEM((N_HEADS, tq, 2 * HEAD_DIM), BF16),
                        pltpu.VMEM((N_HEADS, tq, 2 * HEAD_DIM), F32)],
        compiler_params=_params("arbitrary", "arbitrary"),
        name="fox_attention",
    )(fq, fk, fv)


def _outproj_kernel(x_ref, mod_ref, g1_ref, ret_ref, gla_ref, fox_ref, wbr_ref, wmg_ref, bmg_ref, wo_ref,
                    o_ref):
    d = x_ref.shape[2]
    x = x_ref[0]
    mod = mod_ref[0, 0]
    hb = _norm_modulate(x, g1_ref[0], mod[:, 0:d], mod[:, d:2 * d]).astype(BF16)
    mixed = None
    for n, br_ref in enumerate((ret_ref, gla_ref, fox_ref)):
        y = _dot(br_ref[0], wbr_ref[0, n])
        gate = _sigmoid(_dot(hb, wmg_ref[0, :, n * d:(n + 1) * d]) + bmg_ref[0, :, n * d:(n + 1) * d])
        mixed = gate * y if mixed is None else mixed + gate * y
    out = _dot(mixed.astype(BF16), wo_ref[0])
    o_ref[0] = x + mod[:, 2 * d:3 * d] * out


def _outproj_call(l, x, mod4, g1, ret, gla, fox, wbr, wmg, bmg, wo):
    b, s, d = x.shape
    ts = min(TS_OUT, s)
    tile = lambda i, j: (i, j, 0)
    return pl.pallas_call(
        _outproj_kernel,
        out_shape=jax.ShapeDtypeStruct((b, s, d), F32),
        grid=(b, s // ts),
        in_specs=[
            pl.BlockSpec((1, ts, d), tile),
            pl.BlockSpec((1, 1, 1, mod4.shape[3]), lambda i, j: (l, i, 0, 0)),
            _layer_spec(g1, l),
            pl.BlockSpec((1, ts, BRANCH_W), tile),
            pl.BlockSpec((1, ts, BRANCH_W), tile),
            pl.BlockSpec((1, ts, BRANCH_W), tile),
            _layer_spec(wbr, l),
            _layer_spec(wmg, l),
            _layer_spec(bmg, l),
            _layer_spec(wo, l),
        ],
        out_specs=pl.BlockSpec((1, ts, d), tile),
        compiler_params=_params("arbitrary", "arbitrary"),
        name="out_proj",
    )(x, mod4, g1, ret, gla, fox, wbr, wmg, bmg, wo)


def _ffn_kernel(x_ref, mod_ref, g2_ref, wup_ref, wconv_ref, bconv_ref, wdown_ref, o_ref, tail_ref):
    d = x_ref.shape[2]
    ts = x_ref.shape[1]
    ff = wdown_ref.shape[1]

    @pl.when(pl.program_id(1) == 0)
    def _():
        tail_ref[...] = jnp.zeros_like(tail_ref)

    x = x_ref[0]
    mod = mod_ref[0, 0]
    hb = _norm_modulate(x, g2_ref[0], mod[:, 3 * d:4 * d], mod[:, 4 * d:5 * d]).astype(BF16)
    u = _dot(hb, wup_ref[0, :, 0:ff])
    g = _dot(hb, wup_ref[0, :, ff:2 * ff])
    row = lax.broadcasted_iota(jnp.int32, u.shape, 0)
    prev1 = tail_ref[CONV_W - 2:CONV_W - 1, :]
    prev2 = tail_ref[CONV_W - 3:CONV_W - 2, :]
    u1 = jnp.where(row == 0, prev1, pltpu.roll(u, 1, axis=0))
    u2 = jnp.where(row == 0, prev2, jnp.where(row == 1, prev1, pltpu.roll(u, 2, axis=0)))
    tail_ref[...] = u[ts - (CONV_W - 1):ts, :]
    wc = wconv_ref[0]
    conv = bconv_ref[0] + wc[0:1, :] * u2 + wc[1:2, :] * u1 + wc[2:3, :] * u
    act = (_silu(conv) * g).astype(BF16)
    y = _dot(act, wdown_ref[0])
    o_ref[0] = x + mod[:, 5 * d:6 * d] * y


def _ffn_call(l, x, mod4, g2, wup, wconv, bconv, wdown):
    b, s, d = x.shape
    ts = min(TS_FFN, s)
    ff = wdown.shape[1]
    tile = lambda i, j: (i, j, 0)
    return pl.pallas_call(
        _ffn_kernel,
        out_shape=jax.ShapeDtypeStruct((b, s, d), F32),
        grid=(b, s // ts),
        in_specs=[
            pl.BlockSpec((1, ts, d), tile),
            pl.BlockSpec((1, 1, 1, mod4.shape[3]), lambda i, j: (l, i, 0, 0)),
            _layer_spec(g2, l),
            _layer_spec(wup, l),
            _layer_spec(wconv, l),
            _layer_spec(bconv, l),
            _layer_spec(wdown, l),
        ],
        out_specs=pl.BlockSpec((1, ts, d), tile),
        scratch_shapes=[pltpu.VMEM((CONV_W - 1, ff), F32)],
        compiler_params=_params("arbitrary", "arbitrary"),
        name="conv_ffn",
    )(x, mod4, g2, wup, wconv, bconv, wdown)


def _rope_tables(s):
    half = HEAD_DIM // 2
    pos = jnp.arange(s, dtype=F32)
    inv_freq = ROPE_BASE ** (-jnp.arange(half, dtype=F32) / half)
    ang = pos[:, None] * inv_freq[None, :]
    cos, sin = jnp.cos(ang), jnp.sin(ang)
    return jnp.concatenate([cos, cos], axis=1), jnp.concatenate([-sin, sin], axis=1)


def _retention_tables(ts):
    log_g = jnp.log1p(-jnp.exp2(-5.0 - jnp.arange(N_HEADS, dtype=F32)))[:, None, None]
    t = jnp.arange(ts, dtype=F32)
    diff = t[:, None] - t[None, :]
    same = (jnp.arange(ts)[:, None] // CHUNK) == (jnp.arange(ts)[None, :] // CHUNK)
    expo = jnp.where(same, jnp.abs(diff), diff)[None]
    mask = jnp.where((same | (diff > 0))[None], jnp.exp(jnp.where(expo >= 0, expo, 0.0) * log_g), 0.0)
    ones = jnp.ones((1, 1, HEAD_DIM), F32)
    qw = jnp.exp((t + 1.0)[None, :, None] * log_g) * ones
    kw = jnp.exp((ts - 1.0 - t)[None, :, None] * log_g) * ones
    gts = jnp.exp(ts * log_g) * ones
    return mask, qw, kw, gts


def _chunk_matrices(ts):
    i = jnp.arange(ts)
    same = (i[:, None] // CHUNK) == (i[None, :] // CHUNK)
    btri = (same & (i[:, None] >= i[None, :])).astype(BF16)
    return btri, same.astype(BF16)


def kernel(x, c, norm1_g, norm2_g, w_ada, b_ada, w_in, w_gla_a2, b_gla_a, b_fox_f, ret_norm_g, gla_norm_g,
           q_norm_g, k_norm_g, w_br, w_mg, b_mg, w_o, w_up, w_conv, b_conv, w_down):
    b, s, d = x.shape
    depth = w_ada.shape[0]
    row = lambda a: a[:, None, :]

    mod4 = _ada_call(c, w_ada, b_ada)[:, :, None, :]
    wmain, wsmall = _win_prep_call(w_in.astype(BF16))
    wa2 = jnp.pad(w_gla_a2, ((0, 0), (0, LANES - GLA_LOWRANK), (0, 0))).astype(BF16)
    bfox = jnp.pad(b_fox_f, ((0, 0), (FF_LANE0, LANES - FF_LANE0 - N_HEADS)))
    wbr, wmg, wo, wup, wdown = (a.astype(BF16) for a in (w_br, w_mg, w_o, w_up, w_down))
    g1, g2, ba, bfox, rg, gg, qg, kg, bmg, bconv = (
        row(a) for a in (norm1_g, norm2_g, b_gla_a, bfox, ret_norm_g, gla_norm_g, q_norm_g, k_norm_g,
                         b_mg, b_conv))

    ts_in = min(TS_IN, s)
    ts_mix = min(TS_MIX, s)
    i = jnp.arange(ts_in)
    tri = (i[:, None] >= i[None, :]).astype(BF16)
    cos_t, sin_t = _rope_tables(s)
    mret, qw, kw, gts = _retention_tables(ts_mix)
    btri, bones = _chunk_matrices(ts_mix)

    for l in range(depth):
        r, gl, fq, fk, fv, la = _inproj_call(
            l, x, mod4, g1, wmain, wsmall, wa2, ba, bfox, cos_t, sin_t, qg, kg, tri)
        ret, gla_o = _mixer_call(l, r, gl, la, mret, qw, kw, gts, btri, bones, rg, gg)
        fox = _fox_call(fq, fk, fv, min(TQ_FOX, s))
        x = _outproj_call(l, x, mod4, g1, ret, gla_o, fox, wbr, wmg, bmg, wo)
        x = _ffn_call(l, x, mod4, g2, wup, w_conv, bconv, wdown)
    return x
```

```python
import jax
import jax.numpy as jnp
from jax import lax
from jax.experimental import pallas as pl
from jax.experimental.pallas import tpu as pltpu

F32 = jnp.float32
BF16 = jnp.bfloat16

CHUNK = 64
N_HEADS = 4
HEAD_DIM = 128
BRANCH_W = N_HEADS * HEAD_DIM
GLA_DK = HEAD_DIM // 2
GLA_LOWRANK = 16
GLA_TAU = 16.0
N_BRANCH = 3
CONV_W = 3
ROPE_BASE = 10000.0
EPS = 1e-6

LANES = 128
SUBLANES = 8
FF_LANE0 = GLA_LOWRANK
AUG_ROWMAX0 = 6
LOG2E = 1.4426950408889634
FOX_Q_SCALE = HEAD_DIM ** -0.5 * LOG2E
NORM_SLACK = 1.02
BOUND_LIMIT = 40.0

VMEM_LIMIT = 56 * 1024 * 1024

TS_IN = 512
TS_MIX = 256
TQ_FOX = 512
TS_OUT = 512
TS_FFN = 512
TR_PREP = 256
_NK = N_HEADS * GLA_DK
O_GLR = 4 * BRANCH_W + 2 * _NK + BRANCH_W
O_GG = O_GLR + GLA_LOWRANK
O_FF = O_GG + 4 * BRANCH_W
W_MAIN = O_GLR + (O_FF - O_GG)


def _dot(a, b):
    return jnp.dot(a, b, preferred_element_type=F32)


def _dot_nt(a, b):
    return lax.dot_general(a, b, (((1,), (1,)), ((), ())), preferred_element_type=F32)


def _dot_tn(a, b):
    return lax.dot_general(a, b, (((0,), (0,)), ((), ())), preferred_element_type=F32)


def _split3(x):
    x_hi = x.astype(BF16).astype(F32)
    r1 = x - x_hi
    x_mid = r1.astype(BF16).astype(F32)
    x_lo = (r1 - x_mid).astype(BF16).astype(F32)
    return x_hi, x_mid, x_lo


def _split3_dot(m_bf16, x):
    x_hi, x_mid, x_lo = (t.astype(BF16) for t in _split3(x))
    return _dot(m_bf16, x_hi) + _dot(m_bf16, x_mid) + _dot(m_bf16, x_lo)


def _log_sigmoid(x):
    return jnp.minimum(x, 0.0) - jnp.log1p(jnp.exp(-jnp.abs(x)))


def _sigmoid(x):
    return 1.0 / (1.0 + jnp.exp(-x))


def _silu(x):
    return x * _sigmoid(x)


def _norm_modulate(x, g, shift, scale):
    ms = jnp.mean(x * x, axis=-1, keepdims=True)
    y = x * lax.rsqrt(ms + EPS) * g
    return y * (1.0 + scale) + shift


def _params(*sem):
    return pltpu.CompilerParams(dimension_semantics=sem, vmem_limit_bytes=VMEM_LIMIT)


def _layer_spec(arr, l):
    zeros = (0,) * (arr.ndim - 1)
    return pl.BlockSpec((1,) + arr.shape[1:], lambda *_: (l,) + zeros, pipeline_mode=pl.Buffered(1))


def _const_spec(arr):
    zeros = (0,) * arr.ndim
    return pl.BlockSpec(arr.shape, lambda *_: zeros, pipeline_mode=pl.Buffered(1))


def _ada_kernel(c_ref, w_ref, b_ref, o_ref):
    c_act = _silu(c_ref[...]).astype(BF16)
    o_ref[0] = _dot(c_act, w_ref[0].astype(BF16)) + b_ref[0]


def _ada_call(c, w_ada, b_ada):
    depth, d, n = w_ada.shape
    b = c.shape[0]
    tn = n // 4
    return pl.pallas_call(
        _ada_kernel,
        out_shape=jax.ShapeDtypeStruct((depth, b, n), F32),
        grid=(depth, n // tn),
        in_specs=[
            pl.BlockSpec((b, d), lambda l, j: (0, 0)),
            pl.BlockSpec((1, d, tn), lambda l, j: (l, 0, j)),
            pl.BlockSpec((1, 1, tn), lambda l, j: (l, 0, j)),
        ],
        out_specs=pl.BlockSpec((1, b, tn), lambda l, j: (l, 0, j)),
        compiler_params=_params("arbitrary", "arbitrary"),
        name="ada_mod",
    )(c, w_ada, b_ada.reshape(depth, 1, n))


def _win_prep_kernel(w_ref, main_ref, small_ref):
    tr = w_ref.shape[1]
    main_ref[0, :, 0:O_GLR] = w_ref[0, :, 0:O_GLR].astype(BF16)
    main_ref[0, :, O_GLR:W_MAIN] = w_ref[0, :, O_GG:O_FF].astype(BF16)
    small = jnp.concatenate(
        [w_ref[0, :, O_GLR:O_GG], w_ref[0, :, O_FF:O_FF + N_HEADS],
         jnp.zeros((tr, LANES - GLA_LOWRANK - N_HEADS), w_ref.dtype)], axis=1)
    small_ref[0] = small.astype(BF16)


def _win_prep_call(w_in):
    depth, d, n = w_in.shape
    tr = TR_PREP
    tile = lambda l, i: (l, i, 0)
    return pl.pallas_call(
        _win_prep_kernel,
        out_shape=(jax.ShapeDtypeStruct((depth, d, W_MAIN), BF16),
                   jax.ShapeDtypeStruct((depth, d, LANES), BF16)),
        grid=(depth, d // tr),
        in_specs=[pl.BlockSpec((1, tr, n), tile)],
        out_specs=(pl.BlockSpec((1, tr, W_MAIN), tile), pl.BlockSpec((1, tr, LANES), tile)),
        compiler_params=_params("arbitrary", "arbitrary"),
        name="win_prep",
    )(w_in)


def _inproj_kernel(x_ref, mod_ref, g1_ref, wmain_ref, wsmall_ref, wa2_ref, ba_ref, bfox_ref,
                   cos_ref, sin_ref, qg_ref, kg_ref, tri_ref,
                   r_ref, gl_ref, fq_ref, fk_ref, fv_ref, la_ref, carry_ref):
    d = x_ref.shape[2]
    w = BRANCH_W

    @pl.when(pl.program_id(1) == 0)
    def _():
        carry_ref[...] = jnp.zeros_like(carry_ref)

    x = x_ref[0]
    mod = mod_ref[0, 0]
    h = _norm_modulate(x, g1_ref[0], mod[:, 0:d], mod[:, d:2 * d])
    hb = h.astype(BF16)
    cos = cos_ref[...]
    sin = sin_ref[...]

    def heads_map(acc, fn):
        return jnp.concatenate(
            [fn(acc[:, i * HEAD_DIM:(i + 1) * HEAD_DIM]) for i in range(N_HEADS)], axis=1)

    def rope(t):
        return t * cos + pltpu.roll(t, HEAD_DIM // 2, axis=1) * sin

    def qk_norm(g):
        def fn(t):
            ms = jnp.mean(t * t, axis=-1, keepdims=True)
            return t * lax.rsqrt(ms + EPS) * g
        return fn

    def proj(c0, n):
        return _dot(hb, wmain_ref[0, :, c0:c0 + n])

    small = _dot(hb, wsmall_ref[0])
    c0 = 7 * w
    fq, fk = proj(c0, w), proj(c0 + w, w)
    la_pre = _dot(small.astype(BF16), wa2_ref[0]) + ba_ref[0]
    la_ref[0] = _log_sigmoid(la_pre) * (1.0 / GLA_TAU)
    lane = lax.broadcasted_iota(jnp.int32, small.shape, 1)
    is_ff = (lane >= FF_LANE0) & (lane < FF_LANE0 + N_HEADS)
    log_f = jnp.where(is_ff, _log_sigmoid(small + bfox_ref[0]), 0.0)
    cum = _split3_dot(tri_ref[...], log_f) + carry_ref[...]
    carry_ref[...] = cum[cum.shape[0] - 1:cum.shape[0], :]
    rq, rk = proj(0, w), proj(w, w)

    q_norm, k_norm = qk_norm(qg_ref[0]), qk_norm(kg_ref[0])
    q_ones = jnp.where((lane >= 3) & (lane < AUG_ROWMAX0), 1.0, 0.0)
    k_ones = jnp.where((lane < 3) | ((lane >= AUG_ROWMAX0) & (lane < AUG_ROWMAX0 + 3)), 1.0, 0.0)
    for i in range(N_HEADS):
        sl = slice(i * HEAD_DIM, (i + 1) * HEAD_DIM)
        o = 2 * i * HEAD_DIM
        c = jnp.broadcast_to(cum[:, FF_LANE0 + i:FF_LANE0 + i + 1] * LOG2E, (cum.shape[0], LANES))
        c_hi, c_mid, c_lo = _split3(c)
        q_extra = jnp.where(lane == 0, c_hi, jnp.where(lane == 1, c_mid, jnp.where(lane == 2, c_lo, q_ones)))
        k_extra = jnp.where(lane == 3, -c_hi, jnp.where(lane == 4, -c_mid, jnp.where(lane == 5, -c_lo, k_ones)))
        fq_ref[0, :, o:o + HEAD_DIM] = (q_norm(fq[:, sl]) * FOX_Q_SCALE).astype(BF16)
        fq_ref[0, :, o + HEAD_DIM:o + 2 * HEAD_DIM] = q_extra.astype(BF16)
        fk_ref[0, :, o:o + HEAD_DIM] = k_norm(fk[:, sl]).astype(BF16)
        fk_ref[0, :, o + HEAD_DIM:o + 2 * HEAD_DIM] = k_extra.astype(BF16)

    gqk = proj(4 * w, w)
    r_ref[0, :, 0:w] = heads_map(rq, rope).astype(BF16)
    r_ref[0, :, w:2 * w] = (heads_map(rk, rope) * HEAD_DIM ** -0.5).astype(BF16)
    gqk = jnp.where(lax.broadcasted_iota(jnp.int32, gqk.shape, 1) < _NK, gqk * GLA_DK ** -0.5, gqk)
    gl_ref[0, :, 0:w] = gqk.astype(BF16)
    r_ref[0, :, 3 * w:4 * w] = _silu(proj(3 * w, w)).astype(BF16)
    gl_ref[0, :, 2 * w:3 * w] = _silu(proj(6 * w, w)).astype(BF16)
    r_ref[0, :, 2 * w:3 * w] = proj(2 * w, w).astype(BF16)
    gl_ref[0, :, w:2 * w] = proj(5 * w, w).astype(BF16)
    fv = proj(c0 + 2 * w, w)
    for i in range(N_HEADS):
        o = 2 * i * HEAD_DIM
        fv_ref[0, :, o:o + HEAD_DIM] = fv[:, i * HEAD_DIM:(i + 1) * HEAD_DIM].astype(BF16)
        fv_ref[0, :, o + HEAD_DIM:o + 2 * HEAD_DIM] = jnp.ones((fv.shape[0], HEAD_DIM), BF16)


def _inproj_call(l, x, mod4, g1, wmain, wsmall, wa2, ba, bfox, cos_t, sin_t, qg, kg, tri):
    b, s, d = x.shape
    ts = min(TS_IN, s)
    w = BRANCH_W
    tile = lambda i, j: (i, j, 0)
    return pl.pallas_call(
        _inproj_kernel,
        out_shape=(
            jax.ShapeDtypeStruct((b, s, 4 * w), BF16),
            jax.ShapeDtypeStruct((b, s, 3 * w), BF16),
            jax.ShapeDtypeStruct((b, s, 2 * w), BF16),
            jax.ShapeDtypeStruct((b, s, 2 * w), BF16),
            jax.ShapeDtypeStruct((b, s, 2 * w), BF16),
            jax.ShapeDtypeStruct((b, s, _NK), F32),
        ),
        grid=(b, s // ts),
        in_specs=[
            pl.BlockSpec((1, ts, d), tile),
            pl.BlockSpec((1, 1, 1, mod4.shape[3]), lambda i, j: (l, i, 0, 0)),
            _layer_spec(g1, l),
            _layer_spec(wmain, l),
            _layer_spec(wsmall, l),
            _layer_spec(wa2, l),
            _layer_spec(ba, l),
            _layer_spec(bfox, l),
            pl.BlockSpec((ts, HEAD_DIM), lambda i, j: (j, 0)),
            pl.BlockSpec((ts, HEAD_DIM), lambda i, j: (j, 0)),
            _layer_spec(qg, l),
            _layer_spec(kg, l),
            _const_spec(tri),
        ],
        out_specs=(
            pl.BlockSpec((1, ts, 4 * w), tile),
            pl.BlockSpec((1, ts, 3 * w), tile),
            pl.BlockSpec((1, ts, 2 * w), tile),
            pl.BlockSpec((1, ts, 2 * w), tile),
            pl.BlockSpec((1, ts, 2 * w), tile),
            pl.BlockSpec((1, ts, _NK), tile),
        ),
        scratch_shapes=[pltpu.VMEM((1, LANES), F32)],
        compiler_params=_params("arbitrary", "arbitrary"),
        name="in_proj",
    )(x, mod4, g1, wmain, wsmall, wa2, ba, bfox, cos_t, sin_t, qg, kg, tri)


def _mixer_kernel(r_ref, gl_ref, la_ref, mret_ref, qw_ref, kw_ref, gts_ref, btri_ref, bones_ref,
                  rg_ref, gg_ref, ret_ref, gla_ref, rstate_ref, gstate_ref):
    w = BRANCH_W
    ts = r_ref.shape[1]

    @pl.when(pl.program_id(1) == 0)
    def _():
        rstate_ref[...] = jnp.zeros_like(rstate_ref)
        gstate_ref[...] = jnp.zeros_like(gstate_ref)

    for h in range(N_HEADS):
        sl = slice(h * HEAD_DIM, (h + 1) * HEAD_DIM)
        q = r_ref[0, :, sl]
        k = r_ref[0, :, w + h * HEAD_DIM:w + (h + 1) * HEAD_DIM]
        v = r_ref[0, :, 2 * w + h * HEAD_DIM:2 * w + (h + 1) * HEAD_DIM]
        gate = r_ref[0, :, 3 * w + h * HEAD_DIM:3 * w + (h + 1) * HEAD_DIM].astype(F32)
        state = rstate_ref[h]
        p = (_dot_nt(q, k) * mret_ref[h]).astype(BF16)
        o = _dot(p, v) + _dot(q, state.astype(BF16)) * qw_ref[h]
        kk = (k.astype(F32) * kw_ref[h]).astype(BF16)
        rstate_ref[h] = gts_ref[h] * state + _dot_tn(kk, v)
        mu = jnp.mean(o, axis=-1, keepdims=True)
        oc = o - mu
        var = jnp.mean(oc * oc, axis=-1, keepdims=True)
        on = oc * lax.rsqrt(var + EPS) * rg_ref[0, :, sl]
        ret_ref[0, :, sl] = (gate * on).astype(BF16)

    la = la_ref[0]
    b_cum = _split3_dot(btri_ref[...], la)
    b_end = _split3_dot(bones_ref[...], la)
    kdec = jnp.exp(b_end - b_cum)
    a_all = jnp.exp(b_end)
    kk_all = (gl_ref[0, :, _NK:2 * _NK].astype(F32) * kdec).astype(BF16)
    q_all = gl_ref[0, :, 0:_NK]
    lane = lax.broadcasted_iota(jnp.int32, (ts, LANES), 1)
    chunks = [slice(c * CHUNK, (c + 1) * CHUNK) for c in range(ts // CHUNK)]
    pair_lanes = [slice((h // 2) * LANES, (h // 2 + 1) * LANES) for h in range(N_HEADS)]
    kvs = []
    for h in range(N_HEADS):
        kk = kk_all[:, pair_lanes[h]]
        v = gl_ref[0, :, w + h * HEAD_DIM:w + (h + 1) * HEAD_DIM]
        kvs.append([_dot_tn(v[rows], kk[rows]) for rows in chunks])
    states = []
    for h in range(N_HEADS):
        state = gstate_ref[h]
        per_chunk = []
        for c, rows in enumerate(chunks):
            state = a_all[c * CHUNK:c * CHUNK + 1, pair_lanes[h]] * state + kvs[h][c]
            per_chunk.append(state.astype(BF16))
        gstate_ref[h] = state
        states.append(per_chunk)
    for h in range(N_HEADS):
        own = (lane >= (h % 2) * GLA_DK) & (lane < (h % 2 + 1) * GLA_DK)
        qm = jnp.where(own, q_all[:, pair_lanes[h]], jnp.zeros((), BF16))
        gate = gl_ref[0, :, 2 * w + h * HEAD_DIM:2 * w + (h + 1) * HEAD_DIM].astype(F32)
        o = jnp.concatenate([_dot_nt(qm[rows], states[h][c]) for c, rows in enumerate(chunks)], axis=0)
        ms = jnp.mean(o * o, axis=-1, keepdims=True)
        on = o * lax.rsqrt(ms + EPS) * gg_ref[0]
        gla_ref[0, :, h * HEAD_DIM:(h + 1) * HEAD_DIM] = (gate * on).astype(BF16)


def _mixer_call(l, r, gl, la, mret, qw, kw, gts, btri, bones, rg, gg):
    b, s, _ = r.shape
    ts = mret.shape[1]
    w = BRANCH_W
    tile = lambda i, j: (i, j, 0)
    return pl.pallas_call(
        _mixer_kernel,
        out_shape=(jax.ShapeDtypeStruct((b, s, w), BF16), jax.ShapeDtypeStruct((b, s, w), BF16)),
        grid=(b, s // ts),
        in_specs=[
            pl.BlockSpec((1, ts, 4 * w), tile),
            pl.BlockSpec((1, ts, 3 * w), tile),
            pl.BlockSpec((1, ts, _NK), tile),
            _const_spec(mret),
            _const_spec(qw),
            _const_spec(kw),
            _const_spec(gts),
            _const_spec(btri),
            _const_spec(bones),
            _layer_spec(rg, l),
            _layer_spec(gg, l),
        ],
        out_specs=(pl.BlockSpec((1, ts, w), tile), pl.BlockSpec((1, ts, w), tile)),
        scratch_shapes=[pltpu.VMEM((N_HEADS, HEAD_DIM, HEAD_DIM), F32),
                        pltpu.VMEM((N_HEADS, HEAD_DIM, LANES), F32)],
        compiler_params=_params("arbitrary", "arbitrary"),
        name="ret_gla_mixer",
    )(r, gl, la, mret, qw, kw, gts, btri, bones, rg, gg)


def _fox_kernel(q_ref, k_ref, v_ref, qg_ref, kg_ref, o_ref, mpart_ref, q2_ref, acc_ref):
    tq = q_ref.shape[1]
    qi = pl.program_id(1)
    hw = 2 * HEAD_DIM
    hs = [slice(h * hw, (h + 1) * hw) for h in range(N_HEADS)]
    half = tq // 2
    whole, top, bottom = slice(0, tq), slice(0, half), slice(half, tq)
    top_mask = (lax.broadcasted_iota(jnp.int32, (half, half), 0)
                >= lax.broadcasted_iota(jnp.int32, (half, half), 1))
    bottom_mask = (lax.broadcasted_iota(jnp.int32, (half, tq), 1)
                   <= lax.broadcasted_iota(jnp.int32, (half, tq), 0) + half)

    def scores(lhs, keys, mask):
        out = [_dot_nt(lhs(h), k_ref[0, keys, hs[h]]) for h in range(N_HEADS)]
        if mask is not None:
            out = [jnp.where(mask, s, -jnp.inf) for s in out]
        return out

    def pass1(qrows, start, width, mask):
        keys = pl.ds(pl.multiple_of(start, width), width)
        for h, s in enumerate(scores(lambda h: q_ref[0, qrows, hs[h]], keys, mask)):
            m = mpart_ref[h, qrows]
            for t in range(width // LANES):
                m = jnp.maximum(m, s[:, t * LANES:(t + 1) * LANES])
            mpart_ref[h, qrows] = m

    def pass2(qrows, start, width, mask):
        keys = pl.ds(pl.multiple_of(start, width), width)
        ps = [jnp.exp2(s).astype(BF16) for s in scores(lambda h: q2_ref[h, qrows], keys, mask)]
        for h in range(N_HEADS):
            acc_ref[h, qrows] += _dot(ps[h], v_ref[0, keys, hs[h]])

    def loop(fn):
        def body(jb, carry):
            fn(whole, jb * tq, tq, None)
            return carry
        lax.fori_loop(0, qi, body, 0)
        fn(top, qi * tq, half, top_mask)
        fn(bottom, qi * tq, tq, bottom_mask)

    gain = (jnp.max(jnp.abs(qg_ref[0]), axis=-1, keepdims=True)
            * jnp.max(jnp.abs(kg_ref[0]), axis=-1, keepdims=True))
    bound = gain * (FOX_Q_SCALE * HEAD_DIM * NORM_SLACK)
    bound_is_safe = jnp.max(bound) < BOUND_LIMIT

    @pl.when(bound_is_safe)
    def _():
        mpart_ref[...] = jnp.broadcast_to(bound.reshape(1, 1, 1), mpart_ref.shape)

    @pl.when(jnp.logical_not(bound_is_safe))
    def _():
        mpart_ref[...] = jnp.full(mpart_ref.shape, -jnp.inf, F32)
        loop(pass1)

    lane = lax.broadcasted_iota(jnp.int32, (tq, LANES), 1)
    for h in range(N_HEADS):
        m = jnp.broadcast_to(jnp.max(mpart_ref[h], axis=-1, keepdims=True), (tq, LANES))
        m_hi, m_mid, m_lo = _split3(m)
        extra = q_ref[0, :, h * hw + HEAD_DIM:(h + 1) * hw].astype(F32)
        extra = jnp.where(lane == AUG_ROWMAX0, -m_hi, jnp.where(
            lane == AUG_ROWMAX0 + 1, -m_mid, jnp.where(lane == AUG_ROWMAX0 + 2, -m_lo, extra)))
        q2_ref[h, :, 0:HEAD_DIM] = q_ref[0, :, h * hw:h * hw + HEAD_DIM]
        q2_ref[h, :, HEAD_DIM:hw] = extra.astype(BF16)
    acc_ref[...] = jnp.zeros_like(acc_ref)
    loop(pass2)
    for h in range(N_HEADS):
        acc = acc_ref[h]
        o_ref[0, :, h * HEAD_DIM:(h + 1) * HEAD_DIM] = (acc[:, 0:HEAD_DIM] / acc[:, HEAD_DIM:hw]).astype(BF16)


def _fox_call(l, fq, fk, fv, qg, kg, tq):
    b, s, wide = fq.shape
    assert s % tq == 0
    return pl.pallas_call(
        _fox_kernel,
        out_shape=jax.ShapeDtypeStruct((b, s, BRANCH_W), BF16),
        grid=(b, s // tq),
        in_specs=[
            pl.BlockSpec((1, tq, wide), lambda i, j: (i, j, 0)),
            pl.BlockSpec((1, s, wide), lambda i, j: (i, 0, 0)),
            pl.BlockSpec((1, s, wide), lambda i, j: (i, 0, 0)),
            _layer_spec(qg, l),
            _layer_spec(kg, l),
        ],
        out_specs=pl.BlockSpec((1, tq, BRANCH_W), lambda i, j: (i, j, 0)),
        scratch_shapes=[pltpu.VMEM((N_HEADS, tq, LANES), F32),
                        pltpu.VMEM((N_HEADS, tq, 2 * HEAD_DIM), BF16),
                        pltpu.VMEM((N_HEADS, tq, 2 * HEAD_DIM), F32)],
        compiler_params=_params("arbitrary", "arbitrary"),
        name="fox_attention",
    )(fq, fk, fv, qg, kg)


def _outproj_kernel(x_ref, mod_ref, g1_ref, ret_ref, gla_ref, fox_ref, wbr_ref, wmg_ref, bmg_ref, wo_ref,
                    o_ref):
    d = x_ref.shape[2]
    x = x_ref[0]
    mod = mod_ref[0, 0]
    hb = _norm_modulate(x, g1_ref[0], mod[:, 0:d], mod[:, d:2 * d]).astype(BF16)
    mixed = None
    for n, br_ref in enumerate((ret_ref, gla_ref, fox_ref)):
        y = _dot(br_ref[0], wbr_ref[0, n])
        gate = _sigmoid(_dot(hb, wmg_ref[0, :, n * d:(n + 1) * d]) + bmg_ref[0, :, n * d:(n + 1) * d])
        mixed = gate * y if mixed is None else mixed + gate * y
    out = _dot(mixed.astype(BF16), wo_ref[0])
    o_ref[0] = x + mod[:, 2 * d:3 * d] * out


def _outproj_call(l, x, mod4, g1, ret, gla, fox, wbr, wmg, bmg, wo):
    b, s, d = x.shape
    ts = min(TS_OUT, s)
    tile = lambda i, j: (i, j, 0)
    return pl.pallas_call(
        _outproj_kernel,
        out_shape=jax.ShapeDtypeStruct((b, s, d), F32),
        grid=(b, s // ts),
        in_specs=[
            pl.BlockSpec((1, ts, d), tile),
            pl.BlockSpec((1, 1, 1, mod4.shape[3]), lambda i, j: (l, i, 0, 0)),
            _layer_spec(g1, l),
            pl.BlockSpec((1, ts, BRANCH_W), tile),
            pl.BlockSpec((1, ts, BRANCH_W), tile),
            pl.BlockSpec((1, ts, BRANCH_W), tile),
            _layer_spec(wbr, l),
            _layer_spec(wmg, l),
            _layer_spec(bmg, l),
            _layer_spec(wo, l),
        ],
        out_specs=pl.BlockSpec((1, ts, d), tile),
        compiler_params=_params("arbitrary", "arbitrary"),
        name="out_proj",
    )(x, mod4, g1, ret, gla, fox, wbr, wmg, bmg, wo)


def _ffn_kernel(x_ref, mod_ref, g2_ref, wup_ref, wconv_ref, bconv_ref, wdown_ref, o_ref, tail_ref):
    d = x_ref.shape[2]
    ts = x_ref.shape[1]
    ff = wdown_ref.shape[1]

    @pl.when(pl.program_id(1) == 0)
    def _():
        tail_ref[...] = jnp.zeros_like(tail_ref)

    x = x_ref[0]
    mod = mod_ref[0, 0]
    hb = _norm_modulate(x, g2_ref[0], mod[:, 3 * d:4 * d], mod[:, 4 * d:5 * d]).astype(BF16)
    u = _dot(hb, wup_ref[0, :, 0:ff])
    g = _dot(hb, wup_ref[0, :, ff:2 * ff])
    row = lax.broadcasted_iota(jnp.int32, u.shape, 0)
    prev1 = tail_ref[CONV_W - 2:CONV_W - 1, :]
    prev2 = tail_ref[CONV_W - 3:CONV_W - 2, :]
    u1 = jnp.where(row == 0, prev1, pltpu.roll(u, 1, axis=0))
    u2 = jnp.where(row == 0, prev2, jnp.where(row == 1, prev1, pltpu.roll(u, 2, axis=0)))
    tail_ref[...] = u[ts - (CONV_W - 1):ts, :]
    wc = wconv_ref[0]
    conv = bconv_ref[0] + wc[0:1, :] * u2 + wc[1:2, :] * u1 + wc[2:3, :] * u
    act = (_silu(conv) * g).astype(BF16)
    y = _dot(act, wdown_ref[0])
    o_ref[0] = x + mod[:, 5 * d:6 * d] * y


def _ffn_call(l, x, mod4, g2, wup, wconv, bconv, wdown):
    b, s, d = x.shape
    ts = min(TS_FFN, s)
    ff = wdown.shape[1]
    tile = lambda i, j: (i, j, 0)
    return pl.pallas_call(
        _ffn_kernel,
        out_shape=jax.ShapeDtypeStruct((b, s, d), F32),
        grid=(b, s // ts),
        in_specs=[
            pl.BlockSpec((1, ts, d), tile),
            pl.BlockSpec((1, 1, 1, mod4.shape[3]), lambda i, j: (l, i, 0, 0)),
            _layer_spec(g2, l),
            _layer_spec(wup, l),
            _layer_spec(wconv, l),
            _layer_spec(bconv, l),
            _layer_spec(wdown, l),
        ],
        out_specs=pl.BlockSpec((1, ts, d), tile),
        scratch_shapes=[pltpu.VMEM((CONV_W - 1, ff), F32)],
        compiler_params=_params("arbitrary", "arbitrary"),
        name="conv_ffn",
    )(x, mod4, g2, wup, wconv, bconv, wdown)


def _rope_tables(s):
    half = HEAD_DIM // 2
    pos = jnp.arange(s, dtype=F32)
    inv_freq = ROPE_BASE ** (-jnp.arange(half, dtype=F32) / half)
    ang = pos[:, None] * inv_freq[None, :]
    cos, sin = jnp.cos(ang), jnp.sin(ang)
    return jnp.concatenate([cos, cos], axis=1), jnp.concatenate([-sin, sin], axis=1)


def _retention_tables(ts):
    log_g = jnp.log1p(-jnp.exp2(-5.0 - jnp.arange(N_HEADS, dtype=F32)))[:, None, None]
    t = jnp.arange(ts, dtype=F32)
    diff = t[:, None] - t[None, :]
    same = (jnp.arange(ts)[:, None] // CHUNK) == (jnp.arange(ts)[None, :] // CHUNK)
    expo = jnp.where(same, jnp.abs(diff), diff)[None]
    mask = jnp.where((same | (diff > 0))[None], jnp.exp(jnp.where(expo >= 0, expo, 0.0) * log_g), 0.0)
    ones = jnp.ones((1, 1, HEAD_DIM), F32)
    qw = jnp.exp((t + 1.0)[None, :, None] * log_g) * ones
    kw = jnp.exp((ts - 1.0 - t)[None, :, None] * log_g) * ones
    gts = jnp.exp(ts * log_g) * ones
    return mask, qw, kw, gts


def _chunk_matrices(ts):
    i = jnp.arange(ts)
    same = (i[:, None] // CHUNK) == (i[None, :] // CHUNK)
    btri = (same & (i[:, None] >= i[None, :])).astype(BF16)
    return btri, same.astype(BF16)


def kernel(x, c, norm1_g, norm2_g, w_ada, b_ada, w_in, w_gla_a2, b_gla_a, b_fox_f, ret_norm_g, gla_norm_g,
           q_norm_g, k_norm_g, w_br, w_mg, b_mg, w_o, w_up, w_conv, b_conv, w_down):
    b, s, d = x.shape
    depth = w_ada.shape[0]
    row = lambda a: a[:, None, :]

    mod4 = _ada_call(c, w_ada, b_ada)[:, :, None, :]
    wmain, wsmall = _win_prep_call(w_in.astype(BF16))
    wa2 = jnp.pad(w_gla_a2, ((0, 0), (0, LANES - GLA_LOWRANK), (0, 0))).astype(BF16)
    bfox = jnp.pad(b_fox_f, ((0, 0), (FF_LANE0, LANES - FF_LANE0 - N_HEADS)))
    wbr, wmg, wo, wup, wdown = (a.astype(BF16) for a in (w_br, w_mg, w_o, w_up, w_down))
    g1, g2, ba, bfox, rg, gg, qg, kg, bmg, bconv = (
        row(a) for a in (norm1_g, norm2_g, b_gla_a, bfox, ret_norm_g, gla_norm_g, q_norm_g, k_norm_g,
                         b_mg, b_conv))

    ts_in = min(TS_IN, s)
    ts_mix = min(TS_MIX, s)
    i = jnp.arange(ts_in)
    tri = (i[:, None] >= i[None, :]).astype(BF16)
    cos_t, sin_t = _rope_tables(s)
    mret, qw, kw, gts = _retention_tables(ts_mix)
    btri, bones = _chunk_matrices(ts_mix)

    for l in range(depth):
        r, gl, fq, fk, fv, la = _inproj_call(
            l, x, mod4, g1, wmain, wsmall, wa2, ba, bfox, cos_t, sin_t, qg, kg, tri)
        ret, gla_o = _mixer_call(l, r, gl, la, mret, qw, kw, gts, btri, bones, rg, gg)
        fox = _fox_call(l, fq, fk, fv, qg, kg, min(TQ_FOX, s))
        x = _outproj_call(l, x, mod4, g1, ret, gla_o, fox, wbr, wmg, bmg, wo)
        x = _ffn_call(l, x, mod4, g2, wup, w_conv, bconv, wdown)
    return x
```

```python
import jax
import jax.numpy as jnp
from jax import lax
from jax.experimental import pallas as pl
from jax.experimental.pallas import tpu as pltpu

F32 = jnp.float32
BF16 = jnp.bfloat16

CHUNK = 64
N_HEADS = 4
HEAD_DIM = 128
BRANCH_W = N_HEADS * HEAD_DIM
GLA_DK = HEAD_DIM // 2
GLA_LOWRANK = 16
GLA_TAU = 16.0
N_BRANCH = 3
CONV_W = 3
ROPE_BASE = 10000.0
EPS = 1e-6

LANES = 128
SUBLANES = 8
FF_LANE0 = GLA_LOWRANK
AUG_ROWMAX0 = 6
LOG2E = 1.4426950408889634
FOX_Q_SCALE = HEAD_DIM ** -0.5 * LOG2E
NORM_SLACK = 1.02
BOUND_LIMIT = 40.0

VMEM_LIMIT = 56 * 1024 * 1024

TS_IN = 512
TS_MIX = 256
TQ_FOX = 512
TS_OUT = 512
TS_FFN = 512
TR_PREP = 256
_NK = N_HEADS * GLA_DK
O_GLR = 4 * BRANCH_W + 2 * _NK + BRANCH_W
O_GG = O_GLR + GLA_LOWRANK
O_FF = O_GG + 4 * BRANCH_W
W_MAIN = O_GLR + (O_FF - O_GG)


def _dot(a, b):
    return jnp.dot(a, b, preferred_element_type=F32)


def _dot_nt(a, b):
    return lax.dot_general(a, b, (((1,), (1,)), ((), ())), preferred_element_type=F32)


def _dot_tn(a, b):
    return lax.dot_general(a, b, (((0,), (0,)), ((), ())), preferred_element_type=F32)


def _split3(x):
    x_hi = x.astype(BF16).astype(F32)
    r1 = x - x_hi
    x_mid = r1.astype(BF16).astype(F32)
    x_lo = (r1 - x_mid).astype(BF16).astype(F32)
    return x_hi, x_mid, x_lo


def _split3_dot(m_bf16, x):
    x_hi, x_mid, x_lo = (t.astype(BF16) for t in _split3(x))
    return _dot(m_bf16, x_hi) + _dot(m_bf16, x_mid) + _dot(m_bf16, x_lo)


def _log_sigmoid(x):
    return jnp.minimum(x, 0.0) - jnp.log1p(jnp.exp(-jnp.abs(x)))


def _sigmoid(x):
    return 1.0 / (1.0 + jnp.exp(-x))


def _silu(x):
    return x * _sigmoid(x)


def _norm_modulate(x, g, shift, scale):
    ms = jnp.mean(x * x, axis=-1, keepdims=True)
    return x * lax.rsqrt(ms + EPS) * (g * (1.0 + scale)) + shift


def _params(*sem):
    return pltpu.CompilerParams(dimension_semantics=sem, vmem_limit_bytes=VMEM_LIMIT)


def _layer_spec(arr, l):
    zeros = (0,) * (arr.ndim - 1)
    return pl.BlockSpec((1,) + arr.shape[1:], lambda *_: (l,) + zeros, pipeline_mode=pl.Buffered(1))


def _const_spec(arr):
    zeros = (0,) * arr.ndim
    return pl.BlockSpec(arr.shape, lambda *_: zeros, pipeline_mode=pl.Buffered(1))


def _ada_kernel(c_ref, w_ref, b_ref, o_ref):
    c_act = _silu(c_ref[...]).astype(BF16)
    o_ref[0] = _dot(c_act, w_ref[0].astype(BF16)) + b_ref[0]


def _ada_call(c, w_ada, b_ada):
    depth, d, n = w_ada.shape
    b = c.shape[0]
    tn = n // 4
    return pl.pallas_call(
        _ada_kernel,
        out_shape=jax.ShapeDtypeStruct((depth, b, n), F32),
        grid=(depth, n // tn),
        in_specs=[
            pl.BlockSpec((b, d), lambda l, j: (0, 0)),
            pl.BlockSpec((1, d, tn), lambda l, j: (l, 0, j)),
            pl.BlockSpec((1, 1, tn), lambda l, j: (l, 0, j)),
        ],
        out_specs=pl.BlockSpec((1, b, tn), lambda l, j: (l, 0, j)),
        compiler_params=_params("arbitrary", "arbitrary"),
        name="ada_mod",
    )(c, w_ada, b_ada.reshape(depth, 1, n))


def _win_prep_kernel(w_ref, main_ref, small_ref):
    tr = w_ref.shape[1]
    main_ref[0, :, 0:O_GLR] = w_ref[0, :, 0:O_GLR].astype(BF16)
    main_ref[0, :, O_GLR:W_MAIN] = w_ref[0, :, O_GG:O_FF].astype(BF16)
    small = jnp.concatenate(
        [w_ref[0, :, O_GLR:O_GG], w_ref[0, :, O_FF:O_FF + N_HEADS],
         jnp.zeros((tr, LANES - GLA_LOWRANK - N_HEADS), w_ref.dtype)], axis=1)
    small_ref[0] = small.astype(BF16)


def _win_prep_call(w_in):
    depth, d, n = w_in.shape
    tr = TR_PREP
    tile = lambda l, i: (l, i, 0)
    return pl.pallas_call(
        _win_prep_kernel,
        out_shape=(jax.ShapeDtypeStruct((depth, d, W_MAIN), BF16),
                   jax.ShapeDtypeStruct((depth, d, LANES), BF16)),
        grid=(depth, d // tr),
        in_specs=[pl.BlockSpec((1, tr, n), tile)],
        out_specs=(pl.BlockSpec((1, tr, W_MAIN), tile), pl.BlockSpec((1, tr, LANES), tile)),
        compiler_params=_params("arbitrary", "arbitrary"),
        name="win_prep",
    )(w_in)


def _inproj_kernel(x_ref, mod_ref, g1_ref, wmain_ref, wsmall_ref, wa2_ref, ba_ref, bfox_ref,
                   cos_ref, sin_ref, qg_ref, kg_ref, tri_ref,
                   r_ref, gl_ref, fq_ref, fk_ref, fv_ref, la_ref, carry_ref):
    d = x_ref.shape[2]
    w = BRANCH_W

    @pl.when(pl.program_id(1) == 0)
    def _():
        carry_ref[...] = jnp.zeros_like(carry_ref)

    x = x_ref[0]
    mod = mod_ref[0, 0]
    h = _norm_modulate(x, g1_ref[0], mod[:, 0:d], mod[:, d:2 * d])
    hb = h.astype(BF16)
    cos = cos_ref[...]
    sin = sin_ref[...]

    def heads_map(acc, fn):
        return jnp.concatenate(
            [fn(acc[:, i * HEAD_DIM:(i + 1) * HEAD_DIM]) for i in range(N_HEADS)], axis=1)

    def rope(t):
        return t * cos + pltpu.roll(t, HEAD_DIM // 2, axis=1) * sin

    def qk_norm(g):
        def fn(t):
            ms = jnp.mean(t * t, axis=-1, keepdims=True)
            return t * lax.rsqrt(ms + EPS) * g
        return fn

    def proj(c0, n):
        return _dot(hb, wmain_ref[0, :, c0:c0 + n])

    small = _dot(hb, wsmall_ref[0])
    c0 = 7 * w
    fq, fk = proj(c0, w), proj(c0 + w, w)
    la_pre = _dot(small.astype(BF16), wa2_ref[0]) + ba_ref[0]
    la_ref[0] = _log_sigmoid(la_pre) * (1.0 / GLA_TAU)
    lane = lax.broadcasted_iota(jnp.int32, small.shape, 1)
    is_ff = (lane >= FF_LANE0) & (lane < FF_LANE0 + N_HEADS)
    log_f = jnp.where(is_ff, _log_sigmoid(small + bfox_ref[0]), 0.0)
    cum = _split3_dot(tri_ref[...], log_f) + carry_ref[...]
    carry_ref[...] = cum[cum.shape[0] - 1:cum.shape[0], :]
    rq, rk = proj(0, w), proj(w, w)

    q_norm, k_norm = qk_norm(qg_ref[0]), qk_norm(kg_ref[0])
    q_ones = jnp.where((lane >= 3) & (lane < AUG_ROWMAX0), 1.0, 0.0)
    k_ones = jnp.where((lane < 3) | ((lane >= AUG_ROWMAX0) & (lane < AUG_ROWMAX0 + 3)), 1.0, 0.0)
    for i in range(N_HEADS):
        sl = slice(i * HEAD_DIM, (i + 1) * HEAD_DIM)
        o = 2 * i * HEAD_DIM
        c = jnp.broadcast_to(cum[:, FF_LANE0 + i:FF_LANE0 + i + 1] * LOG2E, (cum.shape[0], LANES))
        c_hi, c_mid, c_lo = _split3(c)
        q_extra = jnp.where(lane == 0, c_hi, jnp.where(lane == 1, c_mid, jnp.where(lane == 2, c_lo, q_ones)))
        k_extra = jnp.where(lane == 3, -c_hi, jnp.where(lane == 4, -c_mid, jnp.where(lane == 5, -c_lo, k_ones)))
        fq_ref[0, :, o:o + HEAD_DIM] = (q_norm(fq[:, sl]) * FOX_Q_SCALE).astype(BF16)
        fq_ref[0, :, o + HEAD_DIM:o + 2 * HEAD_DIM] = q_extra.astype(BF16)
        fk_ref[0, :, o:o + HEAD_DIM] = k_norm(fk[:, sl]).astype(BF16)
        fk_ref[0, :, o + HEAD_DIM:o + 2 * HEAD_DIM] = k_extra.astype(BF16)

    gqk = proj(4 * w, w)
    r_ref[0, :, 0:w] = heads_map(rq, rope).astype(BF16)
    r_ref[0, :, w:2 * w] = (heads_map(rk, rope) * HEAD_DIM ** -0.5).astype(BF16)
    gqk = jnp.where(lax.broadcasted_iota(jnp.int32, gqk.shape, 1) < _NK, gqk * GLA_DK ** -0.5, gqk)
    gl_ref[0, :, 0:w] = gqk.astype(BF16)
    r_ref[0, :, 3 * w:4 * w] = _silu(proj(3 * w, w)).astype(BF16)
    gl_ref[0, :, 2 * w:3 * w] = _silu(proj(6 * w, w)).astype(BF16)
    r_ref[0, :, 2 * w:3 * w] = proj(2 * w, w).astype(BF16)
    gl_ref[0, :, w:2 * w] = proj(5 * w, w).astype(BF16)
    fv = proj(c0 + 2 * w, w)
    for i in range(N_HEADS):
        o = 2 * i * HEAD_DIM
        fv_ref[0, :, o:o + HEAD_DIM] = fv[:, i * HEAD_DIM:(i + 1) * HEAD_DIM].astype(BF16)
        fv_ref[0, :, o + HEAD_DIM:o + 2 * HEAD_DIM] = jnp.ones((fv.shape[0], HEAD_DIM), BF16)


def _inproj_call(l, x, mod4, g1, wmain, wsmall, wa2, ba, bfox, cos_t, sin_t, qg, kg, tri):
    b, s, d = x.shape
    ts = min(TS_IN, s)
    w = BRANCH_W
    tile = lambda i, j: (i, j, 0)
    return pl.pallas_call(
        _inproj_kernel,
        out_shape=(
            jax.ShapeDtypeStruct((b, s, 4 * w), BF16),
            jax.ShapeDtypeStruct((b, s, 3 * w), BF16),
            jax.ShapeDtypeStruct((b, s, 2 * w), BF16),
            jax.ShapeDtypeStruct((b, s, 2 * w), BF16),
            jax.ShapeDtypeStruct((b, s, 2 * w), BF16),
            jax.ShapeDtypeStruct((b, s, _NK), F32),
        ),
        grid=(b, s // ts),
        in_specs=[
            pl.BlockSpec((1, ts, d), tile),
            pl.BlockSpec((1, 1, 1, mod4.shape[3]), lambda i, j: (l, i, 0, 0)),
            _layer_spec(g1, l),
            _layer_spec(wmain, l),
            _layer_spec(wsmall, l),
            _layer_spec(wa2, l),
            _layer_spec(ba, l),
            _layer_spec(bfox, l),
            pl.BlockSpec((ts, HEAD_DIM), lambda i, j: (j, 0)),
            pl.BlockSpec((ts, HEAD_DIM), lambda i, j: (j, 0)),
            _layer_spec(qg, l),
            _layer_spec(kg, l),
            _const_spec(tri),
        ],
        out_specs=(
            pl.BlockSpec((1, ts, 4 * w), tile),
            pl.BlockSpec((1, ts, 3 * w), tile),
            pl.BlockSpec((1, ts, 2 * w), tile),
            pl.BlockSpec((1, ts, 2 * w), tile),
            pl.BlockSpec((1, ts, 2 * w), tile),
            pl.BlockSpec((1, ts, _NK), tile),
        ),
        scratch_shapes=[pltpu.VMEM((1, LANES), F32)],
        compiler_params=_params("arbitrary", "arbitrary"),
        name="in_proj",
    )(x, mod4, g1, wmain, wsmall, wa2, ba, bfox, cos_t, sin_t, qg, kg, tri)


def _mixer_kernel(r_ref, gl_ref, la_ref, mret_ref, qw_ref, kw_ref, gts_ref, btri_ref, bones_ref,
                  rg_ref, gg_ref, ret_ref, gla_ref, rstate_ref, gstate_ref):
    w = BRANCH_W
    ts = r_ref.shape[1]

    @pl.when(pl.program_id(1) == 0)
    def _():
        rstate_ref[...] = jnp.zeros_like(rstate_ref)
        gstate_ref[...] = jnp.zeros_like(gstate_ref)

    def retention_head(h):
        sl = slice(h * HEAD_DIM, (h + 1) * HEAD_DIM)
        q = r_ref[0, :, sl]
        k = r_ref[0, :, w + h * HEAD_DIM:w + (h + 1) * HEAD_DIM]
        v = r_ref[0, :, 2 * w + h * HEAD_DIM:2 * w + (h + 1) * HEAD_DIM]
        gate = r_ref[0, :, 3 * w + h * HEAD_DIM:3 * w + (h + 1) * HEAD_DIM].astype(F32)
        state = rstate_ref[h]
        p = (_dot_nt(q, k) * mret_ref[h]).astype(BF16)
        o = _dot(p, v) + _dot(q, state.astype(BF16)) * qw_ref[h]
        kk = (k.astype(F32) * kw_ref[h]).astype(BF16)
        rstate_ref[h] = gts_ref[h] * state + _dot_tn(kk, v)
        mu = jnp.mean(o, axis=-1, keepdims=True)
        oc = o - mu
        var = jnp.mean(oc * oc, axis=-1, keepdims=True)
        on = oc * lax.rsqrt(var + EPS) * rg_ref[0, :, sl]
        ret_ref[0, :, sl] = (gate * on).astype(BF16)

    la = la_ref[0]
    b_cum = _split3_dot(btri_ref[...], la)
    b_end = _split3_dot(bones_ref[...], la)
    retention_head(0)
    kdec = jnp.exp(b_end - b_cum)
    a_all = jnp.exp(b_end)
    kk_all = (gl_ref[0, :, _NK:2 * _NK].astype(F32) * kdec).astype(BF16)
    q_all = gl_ref[0, :, 0:_NK]
    lane = lax.broadcasted_iota(jnp.int32, (ts, LANES), 1)
    chunks = [slice(c * CHUNK, (c + 1) * CHUNK) for c in range(ts // CHUNK)]
    pair_lanes = [slice((h // 2) * LANES, (h // 2 + 1) * LANES) for h in range(N_HEADS)]
    kvs = []
    for h in range(N_HEADS):
        kk = kk_all[:, pair_lanes[h]]
        v = gl_ref[0, :, w + h * HEAD_DIM:w + (h + 1) * HEAD_DIM]
        kvs.append([_dot_tn(v[rows], kk[rows]) for rows in chunks])
    retention_head(1)
    states = []
    for h in range(N_HEADS):
        state = gstate_ref[h]
        per_chunk = []
        for c, rows in enumerate(chunks):
            state = a_all[c * CHUNK:c * CHUNK + 1, pair_lanes[h]] * state + kvs[h][c]
            per_chunk.append(state.astype(BF16))
        gstate_ref[h] = state
        states.append(per_chunk)
    retention_head(2)
    for h in range(N_HEADS):
        if h == N_HEADS // 2:
            retention_head(3)
        own = (lane >= (h % 2) * GLA_DK) & (lane < (h % 2 + 1) * GLA_DK)
        qm = jnp.where(own, q_all[:, pair_lanes[h]], jnp.zeros((), BF16))
        gate = gl_ref[0, :, 2 * w + h * HEAD_DIM:2 * w + (h + 1) * HEAD_DIM].astype(F32)
        o = jnp.concatenate([_dot_nt(qm[rows], states[h][c]) for c, rows in enumerate(chunks)], axis=0)
        ms = jnp.mean(o * o, axis=-1, keepdims=True)
        on = o * lax.rsqrt(ms + EPS) * gg_ref[0]
        gla_ref[0, :, h * HEAD_DIM:(h + 1) * HEAD_DIM] = (gate * on).astype(BF16)


def _mixer_call(l, r, gl, la, mret, qw, kw, gts, btri, bones, rg, gg):
    b, s, _ = r.shape
    ts = mret.shape[1]
    w = BRANCH_W
    tile = lambda i, j: (i, j, 0)
    return pl.pallas_call(
        _mixer_kernel,
        out_shape=(jax.ShapeDtypeStruct((b, s, w), BF16), jax.ShapeDtypeStruct((b, s, w), BF16)),
        grid=(b, s // ts),
        in_specs=[
            pl.BlockSpec((1, ts, 4 * w), tile),
            pl.BlockSpec((1, ts, 3 * w), tile),
            pl.BlockSpec((1, ts, _NK), tile),
            _const_spec(mret),
            _const_spec(qw),
            _const_spec(kw),
            _const_spec(gts),
            _const_spec(btri),
            _const_spec(bones),
            _layer_spec(rg, l),
            _layer_spec(gg, l),
        ],
        out_specs=(pl.BlockSpec((1, ts, w), tile), pl.BlockSpec((1, ts, w), tile)),
        scratch_shapes=[pltpu.VMEM((N_HEADS, HEAD_DIM, HEAD_DIM), F32),
                        pltpu.VMEM((N_HEADS, HEAD_DIM, LANES), F32)],
        compiler_params=_params("arbitrary", "arbitrary"),
        name="ret_gla_mixer",
    )(r, gl, la, mret, qw, kw, gts, btri, bones, rg, gg)


def _fox_kernel(q_ref, k_ref, v_ref, qg_ref, kg_ref, o_ref, mpart_ref, q2_ref, acc_ref):
    tq = q_ref.shape[1]
    qi = pl.program_id(1)
    hw = 2 * HEAD_DIM
    hs = [slice(h * hw, (h + 1) * hw) for h in range(N_HEADS)]
    half = tq // 2
    whole, top, bottom = slice(0, tq), slice(0, half), slice(half, tq)
    top_mask = (lax.broadcasted_iota(jnp.int32, (half, half), 0)
                >= lax.broadcasted_iota(jnp.int32, (half, half), 1))
    bottom_mask = (lax.broadcasted_iota(jnp.int32, (half, tq), 1)
                   <= lax.broadcasted_iota(jnp.int32, (half, tq), 0) + half)

    def scores(lhs, keys, mask):
        out = [_dot_nt(lhs(h), k_ref[0, keys, hs[h]]) for h in range(N_HEADS)]
        if mask is not None:
            out = [jnp.where(mask, s, -jnp.inf) for s in out]
        return out

    def pass1(qrows, start, width, mask):
        keys = pl.ds(pl.multiple_of(start, width), width)
        for h, s in enumerate(scores(lambda h: q_ref[0, qrows, hs[h]], keys, mask)):
            m = mpart_ref[h, qrows]
            for t in range(width // LANES):
                m = jnp.maximum(m, s[:, t * LANES:(t + 1) * LANES])
            mpart_ref[h, qrows] = m

    def pass2(qrows, start, width, mask):
        keys = pl.ds(pl.multiple_of(start, width), width)
        ps = [jnp.exp2(s).astype(BF16) for s in scores(lambda h: q2_ref[h, qrows], keys, mask)]
        for h in range(N_HEADS):
            acc_ref[h, qrows] += _dot(ps[h], v_ref[0, keys, hs[h]])

    def loop(fn):
        def body(jb, carry):
            fn(whole, jb * tq, tq, None)
            return carry
        lax.fori_loop(0, qi, body, 0)
        fn(top, qi * tq, half, top_mask)
        fn(bottom, qi * tq, tq, bottom_mask)

    gain = (jnp.max(jnp.abs(qg_ref[0]), axis=-1, keepdims=True)
            * jnp.max(jnp.abs(kg_ref[0]), axis=-1, keepdims=True))
    bound = gain * (FOX_Q_SCALE * HEAD_DIM * NORM_SLACK)
    bound_is_safe = jnp.max(bound) < BOUND_LIMIT

    @pl.when(bound_is_safe)
    def _():
        mpart_ref[...] = jnp.broadcast_to(bound.reshape(1, 1, 1), mpart_ref.shape)

    @pl.when(jnp.logical_not(bound_is_safe))
    def _():
        mpart_ref[...] = jnp.full(mpart_ref.shape, -jnp.inf, F32)
        loop(pass1)

    lane = lax.broadcasted_iota(jnp.int32, (tq, LANES), 1)
    for h in range(N_HEADS):
        m = jnp.broadcast_to(jnp.max(mpart_ref[h], axis=-1, keepdims=True), (tq, LANES))
        m_hi, m_mid, m_lo = _split3(m)
        extra = q_ref[0, :, h * hw + HEAD_DIM:(h + 1) * hw].astype(F32)
        extra = jnp.where(lane == AUG_ROWMAX0, -m_hi, jnp.where(
            lane == AUG_ROWMAX0 + 1, -m_mid, jnp.where(lane == AUG_ROWMAX0 + 2, -m_lo, extra)))
        q2_ref[h, :, 0:HEAD_DIM] = q_ref[0, :, h * hw:h * hw + HEAD_DIM]
        q2_ref[h, :, HEAD_DIM:hw] = extra.astype(BF16)
    acc_ref[...] = jnp.zeros_like(acc_ref)
    loop(pass2)
    for h in range(N_HEADS):
        acc = acc_ref[h]
        o_ref[0, :, h * HEAD_DIM:(h + 1) * HEAD_DIM] = (acc[:, 0:HEAD_DIM] / acc[:, HEAD_DIM:hw]).astype(BF16)


def _fox_call(l, fq, fk, fv, qg, kg, tq):
    b, s, wide = fq.shape
    assert s % tq == 0
    return pl.pallas_call(
        _fox_kernel,
        out_shape=jax.ShapeDtypeStruct((b, s, BRANCH_W), BF16),
        grid=(b, s // tq),
        in_specs=[
            pl.BlockSpec((1, tq, wide), lambda i, j: (i, j, 0)),
            pl.BlockSpec((1, s, wide), lambda i, j: (i, 0, 0)),
            pl.BlockSpec((1, s, wide), lambda i, j: (i, 0, 0)),
            _layer_spec(qg, l),
            _layer_spec(kg, l),
        ],
        out_specs=pl.BlockSpec((1, tq, BRANCH_W), lambda i, j: (i, j, 0)),
        scratch_shapes=[pltpu.VMEM((N_HEADS, tq, LANES), F32),
                        pltpu.VMEM((N_HEADS, tq, 2 * HEAD_DIM), BF16),
                        pltpu.VMEM((N_HEADS, tq, 2 * HEAD_DIM), F32)],
        compiler_params=_params("arbitrary", "arbitrary"),
        name="fox_attention",
    )(fq, fk, fv, qg, kg)


def _outproj_kernel(x_ref, mod_ref, g1_ref, ret_ref, gla_ref, fox_ref, wbr_ref, wmg_ref, bmg_ref, wo_ref,
                    o_ref):
    d = x_ref.shape[2]
    x = x_ref[0]
    mod = mod_ref[0, 0]
    hb = _norm_modulate(x, g1_ref[0], mod[:, 0:d], mod[:, d:2 * d]).astype(BF16)
    mixed = None
    for n, br_ref in enumerate((ret_ref, gla_ref, fox_ref)):
        y = _dot(br_ref[0], wbr_ref[0, n])
        gate = _sigmoid(_dot(hb, wmg_ref[0, :, n * d:(n + 1) * d]) + bmg_ref[0, :, n * d:(n + 1) * d])
        mixed = gate * y if mixed is None else mixed + gate * y
    out = _dot(mixed.astype(BF16), wo_ref[0])
    o_ref[0] = x + mod[:, 2 * d:3 * d] * out


def _outproj_call(l, x, mod4, g1, ret, gla, fox, wbr, wmg, bmg, wo):
    b, s, d = x.shape
    ts = min(TS_OUT, s)
    tile = lambda i, j: (i, j, 0)
    return pl.pallas_call(
        _outproj_kernel,
        out_shape=jax.ShapeDtypeStruct((b, s, d), F32),
        grid=(b, s // ts),
        in_specs=[
            pl.BlockSpec((1, ts, d), tile),
            pl.BlockSpec((1, 1, 1, mod4.shape[3]), lambda i, j: (l, i, 0, 0)),
            _layer_spec(g1, l),
            pl.BlockSpec((1, ts, BRANCH_W), tile),
            pl.BlockSpec((1, ts, BRANCH_W), tile),
            pl.BlockSpec((1, ts, BRANCH_W), tile),
            _layer_spec(wbr, l),
            _layer_spec(wmg, l),
            _layer_spec(bmg, l),
            _layer_spec(wo, l),
        ],
        out_specs=pl.BlockSpec((1, ts, d), tile),
        compiler_params=_params("arbitrary", "arbitrary"),
        name="out_proj",
    )(x, mod4, g1, ret, gla, fox, wbr, wmg, bmg, wo)


def _ffn_kernel(x_ref, mod_ref, g2_ref, wup_ref, wconv_ref, bconv_ref, wdown_ref, o_ref, tail_ref):
    d = x_ref.shape[2]
    ts = x_ref.shape[1]
    ff = wdown_ref.shape[1]

    @pl.when(pl.program_id(1) == 0)
    def _():
        tail_ref[...] = jnp.zeros_like(tail_ref)

    x = x_ref[0]
    mod = mod_ref[0, 0]
    hb = _norm_modulate(x, g2_ref[0], mod[:, 3 * d:4 * d], mod[:, 4 * d:5 * d]).astype(BF16)
    u = _dot(hb, wup_ref[0, :, 0:ff])
    g = _dot(hb, wup_ref[0, :, ff:2 * ff])
    row = lax.broadcasted_iota(jnp.int32, u.shape, 0)
    prev1 = tail_ref[CONV_W - 2:CONV_W - 1, :]
    prev2 = tail_ref[CONV_W - 3:CONV_W - 2, :]
    u1 = jnp.where(row == 0, prev1, pltpu.roll(u, 1, axis=0))
    u2 = jnp.where(row == 0, prev2, jnp.where(row == 1, prev1, pltpu.roll(u, 2, axis=0)))
    tail_ref[...] = u[ts - (CONV_W - 1):ts, :]
    wc = wconv_ref[0]
    conv = bconv_ref[0] + wc[0:1, :] * u2 + wc[1:2, :] * u1 + wc[2:3, :] * u
    act = (_silu(conv) * g).astype(BF16)
    y = _dot(act, wdown_ref[0])
    o_ref[0] = x + mod[:, 5 * d:6 * d] * y


def _ffn_call(l, x, mod4, g2, wup, wconv, bconv, wdown):
    b, s, d = x.shape
    ts = min(TS_FFN, s)
    ff = wdown.shape[1]
    tile = lambda i, j: (i, j, 0)
    return pl.pallas_call(
        _ffn_kernel,
        out_shape=jax.ShapeDtypeStruct((b, s, d), F32),
        grid=(b, s // ts),
        in_specs=[
            pl.BlockSpec((1, ts, d), tile),
            pl.BlockSpec((1, 1, 1, mod4.shape[3]), lambda i, j: (l, i, 0, 0)),
            _layer_spec(g2, l),
            _layer_spec(wup, l),
            _layer_spec(wconv, l),
            _layer_spec(bconv, l),
            _layer_spec(wdown, l),
        ],
        out_specs=pl.BlockSpec((1, ts, d), tile),
        scratch_shapes=[pltpu.VMEM((CONV_W - 1, ff), F32)],
        compiler_params=_params("arbitrary", "arbitrary"),
        name="conv_ffn",
    )(x, mod4, g2, wup, wconv, bconv, wdown)


def _rope_tables(s):
    half = HEAD_DIM // 2
    pos = jnp.arange(s, dtype=F32)
    inv_freq = ROPE_BASE ** (-jnp.arange(half, dtype=F32) / half)
    ang = pos[:, None] * inv_freq[None, :]
    cos, sin = jnp.cos(ang), jnp.sin(ang)
    return jnp.concatenate([cos, cos], axis=1), jnp.concatenate([-sin, sin], axis=1)


def _retention_tables(ts):
    log_g = jnp.log1p(-jnp.exp2(-5.0 - jnp.arange(N_HEADS, dtype=F32)))[:, None, None]
    t = jnp.arange(ts, dtype=F32)
    diff = t[:, None] - t[None, :]
    same = (jnp.arange(ts)[:, None] // CHUNK) == (jnp.arange(ts)[None, :] // CHUNK)
    expo = jnp.where(same, jnp.abs(diff), diff)[None]
    mask = jnp.where((same | (diff > 0))[None], jnp.exp(jnp.where(expo >= 0, expo, 0.0) * log_g), 0.0)
    ones = jnp.ones((1, 1, HEAD_DIM), F32)
    qw = jnp.exp((t + 1.0)[None, :, None] * log_g) * ones
    kw = jnp.exp((ts - 1.0 - t)[None, :, None] * log_g) * ones
    gts = jnp.exp(ts * log_g) * ones
    return mask, qw, kw, gts


def _chunk_matrices(ts):
    i = jnp.arange(ts)
    same = (i[:, None] // CHUNK) == (i[None, :] // CHUNK)
    btri = (same & (i[:, None] >= i[None, :])).astype(BF16)
    return btri, same.astype(BF16)


def kernel(x, c, norm1_g, norm2_g, w_ada, b_ada, w_in, w_gla_a2, b_gla_a, b_fox_f, ret_norm_g, gla_norm_g,
           q_norm_g, k_norm_g, w_br, w_mg, b_mg, w_o, w_up, w_conv, b_conv, w_down):
    b, s, d = x.shape
    depth = w_ada.shape[0]
    row = lambda a: a[:, None, :]

    mod4 = _ada_call(c, w_ada, b_ada)[:, :, None, :]
    wmain, wsmall = _win_prep_call(w_in.astype(BF16))
    wa2 = jnp.pad(w_gla_a2, ((0, 0), (0, LANES - GLA_LOWRANK), (0, 0))).astype(BF16)
    bfox = jnp.pad(b_fox_f, ((0, 0), (FF_LANE0, LANES - FF_LANE0 - N_HEADS)))
    wbr, wmg, wo, wup, wdown = (a.astype(BF16) for a in (w_br, w_mg, w_o, w_up, w_down))
    g1, g2, ba, bfox, rg, gg, qg, kg, bmg, bconv = (
        row(a) for a in (norm1_g, norm2_g, b_gla_a, bfox, ret_norm_g, gla_norm_g, q_norm_g, k_norm_g,
                         b_mg, b_conv))

    ts_in = min(TS_IN, s)
    ts_mix = min(TS_MIX, s)
    i = jnp.arange(ts_in)
    tri = (i[:, None] >= i[None, :]).astype(BF16)
    cos_t, sin_t = _rope_tables(s)
    mret, qw, kw, gts = _retention_tables(ts_mix)
    btri, bones = _chunk_matrices(ts_mix)

    for l in range(depth):
        r, gl, fq, fk, fv, la = _inproj_call(
            l, x, mod4, g1, wmain, wsmall, wa2, ba, bfox, cos_t, sin_t, qg, kg, tri)
        ret, gla_o = _mixer_call(l, r, gl, la, mret, qw, kw, gts, btri, bones, rg, gg)
        fox = _fox_call(l, fq, fk, fv, qg, kg, min(TQ_FOX, s))
        x = _outproj_call(l, x, mod4, g1, ret, gla_o, fox, wbr, wmg, bmg, wo)
        x = _ffn_call(l, x, mod4, g2, wup, w_conv, bconv, wdown)
    return x
```

```python
import jax
import jax.numpy as jnp
from jax import lax
from jax.experimental import pallas as pl
from jax.experimental.pallas import tpu as pltpu

F32 = jnp.float32
BF16 = jnp.bfloat16

CHUNK = 64
N_HEADS = 4
HEAD_DIM = 128
BRANCH_W = N_HEADS * HEAD_DIM
GLA_DK = HEAD_DIM // 2
GLA_LOWRANK = 16
GLA_TAU = 16.0
CONV_W = 3
ROPE_BASE = 10000.0
EPS = 1e-6

LANES = 128
FF_LANE0 = GLA_LOWRANK
AUG_T0 = 0
AUG_S0 = 3
AUG_ROWMAX0 = 6
LOG2E = 1.4426950408889634
FOX_Q_SCALE = HEAD_DIM ** -0.5 * LOG2E
NORM_SLACK = 1.02
BOUND_LIMIT = 40.0

VMEM_LIMIT = 56 * 1024 * 1024

TS_IN = 512
TS_MIX = 256
TQ_FOX = 512
TS_OUT = 512
TS_FFN = 512
TR_PREP = 256
_NK = N_HEADS * GLA_DK
O_GLR = 4 * BRANCH_W + 2 * _NK + BRANCH_W
O_GG = O_GLR + GLA_LOWRANK
O_FF = O_GG + 4 * BRANCH_W
W_MAIN = O_GLR + (O_FF - O_GG)


def _dot(a, b):
    return jnp.dot(a, b, preferred_element_type=F32)


def _dot_nt(a, b):
    return lax.dot_general(a, b, (((1,), (1,)), ((), ())), preferred_element_type=F32)


def _dot_tn(a, b):
    return lax.dot_general(a, b, (((0,), (0,)), ((), ())), preferred_element_type=F32)


def _split3(x):
    x_hi = x.astype(BF16).astype(F32)
    r1 = x - x_hi
    x_mid = r1.astype(BF16).astype(F32)
    x_lo = (r1 - x_mid).astype(BF16).astype(F32)
    return x_hi, x_mid, x_lo


def _split3_dot(m_bf16, x):
    x_hi, x_mid, x_lo = (t.astype(BF16) for t in _split3(x))
    return _dot(m_bf16, x_hi) + _dot(m_bf16, x_mid) + _dot(m_bf16, x_lo)


def _log_sigmoid(x):
    return jnp.minimum(x, 0.0) - jnp.log1p(jnp.exp(-jnp.abs(x)))


def _sigmoid(x):
    return 1.0 / (1.0 + jnp.exp(-x))


def _silu(x):
    return x * _sigmoid(x)


def _norm_modulate(x, g, shift, scale):
    ms = jnp.mean(x * x, axis=-1, keepdims=True)
    return x * lax.rsqrt(ms + EPS) * (g * (1.0 + scale)) + shift


def _params(*sem):
    return pltpu.CompilerParams(dimension_semantics=sem, vmem_limit_bytes=VMEM_LIMIT)


def _layer_spec(arr, l):
    zeros = (0,) * (arr.ndim - 1)
    return pl.BlockSpec((1,) + arr.shape[1:], lambda *_: (l,) + zeros, pipeline_mode=pl.Buffered(1))


def _const_spec(arr):
    zeros = (0,) * arr.ndim
    return pl.BlockSpec(arr.shape, lambda *_: zeros, pipeline_mode=pl.Buffered(1))


def _ada_kernel(c_ref, w_ref, b_ref, o_ref):
    c_act = _silu(c_ref[...]).astype(BF16)
    o_ref[0] = _dot(c_act, w_ref[0].astype(BF16)) + b_ref[0]


def _ada_call(c, w_ada, b_ada):
    depth, d, n = w_ada.shape
    b = c.shape[0]
    tn = n // 4
    return pl.pallas_call(
        _ada_kernel,
        out_shape=jax.ShapeDtypeStruct((depth, b, n), F32),
        grid=(depth, n // tn),
        in_specs=[
            pl.BlockSpec((b, d), lambda l, j: (0, 0)),
            pl.BlockSpec((1, d, tn), lambda l, j: (l, 0, j)),
            pl.BlockSpec((1, 1, tn), lambda l, j: (l, 0, j)),
        ],
        out_specs=pl.BlockSpec((1, b, tn), lambda l, j: (l, 0, j)),
        compiler_params=_params("arbitrary", "arbitrary"),
        name="ada_mod",
    )(c, w_ada, b_ada.reshape(depth, 1, n))


def _win_prep_kernel(w_ref, main_ref, small_ref):
    tr = w_ref.shape[1]
    main_ref[0, :, 0:O_GLR] = w_ref[0, :, 0:O_GLR].astype(BF16)
    main_ref[0, :, O_GLR:W_MAIN] = w_ref[0, :, O_GG:O_FF].astype(BF16)
    small = jnp.concatenate(
        [w_ref[0, :, O_GLR:O_GG], w_ref[0, :, O_FF:O_FF + N_HEADS],
         jnp.zeros((tr, LANES - GLA_LOWRANK - N_HEADS), w_ref.dtype)], axis=1)
    small_ref[0] = small.astype(BF16)


def _win_prep_call(w_in):
    depth, d, n = w_in.shape
    tr = TR_PREP
    tile = lambda l, i: (l, i, 0)
    return pl.pallas_call(
        _win_prep_kernel,
        out_shape=(jax.ShapeDtypeStruct((depth, d, W_MAIN), BF16),
                   jax.ShapeDtypeStruct((depth, d, LANES), BF16)),
        grid=(depth, d // tr),
        in_specs=[pl.BlockSpec((1, tr, n), tile)],
        out_specs=(pl.BlockSpec((1, tr, W_MAIN), tile), pl.BlockSpec((1, tr, LANES), tile)),
        compiler_params=_params("arbitrary", "arbitrary"),
        name="win_prep",
    )(w_in)


def _inproj_kernel(x_ref, mod_ref, g1_ref, wmain_ref, wsmall_ref, wa2_ref, ba_ref, bfox_ref,
                   cos_ref, sin_ref, qg_ref, kg_ref, tri_ref,
                   r_ref, gl_ref, fq_ref, fk_ref, fv_ref, la_ref, carry_ref):
    d = x_ref.shape[2]
    w = BRANCH_W

    @pl.when(pl.program_id(1) == 0)
    def _():
        carry_ref[...] = jnp.zeros_like(carry_ref)

    x = x_ref[0]
    mod = mod_ref[0, 0]
    h = _norm_modulate(x, g1_ref[0], mod[:, 0:d], mod[:, d:2 * d])
    hb = h.astype(BF16)
    cos = cos_ref[...]
    sin = sin_ref[...]

    def heads_map(acc, fn):
        return jnp.concatenate(
            [fn(acc[:, i * HEAD_DIM:(i + 1) * HEAD_DIM]) for i in range(N_HEADS)], axis=1)

    def rope(t):
        return t * cos + pltpu.roll(t, HEAD_DIM // 2, axis=1) * sin

    def qk_norm(g):
        def fn(t):
            ms = jnp.mean(t * t, axis=-1, keepdims=True)
            return t * lax.rsqrt(ms + EPS) * g
        return fn

    def proj(c0, n):
        return _dot(hb, wmain_ref[0, :, c0:c0 + n])

    small = _dot(hb, wsmall_ref[0])
    c0 = 7 * w
    fq, fk = proj(c0, w), proj(c0 + w, w)
    la_pre = _dot(small.astype(BF16), wa2_ref[0]) + ba_ref[0]
    la_ref[0] = _log_sigmoid(la_pre) * (1.0 / GLA_TAU)
    lane = lax.broadcasted_iota(jnp.int32, small.shape, 1)
    is_ff = (lane >= FF_LANE0) & (lane < FF_LANE0 + N_HEADS)
    log_f = jnp.where(is_ff, _log_sigmoid(small + bfox_ref[0]), 0.0)
    cum = _split3_dot(tri_ref[...], log_f) + carry_ref[...]
    carry_ref[...] = cum[cum.shape[0] - 1:cum.shape[0], :]
    rq, rk = proj(0, w), proj(w, w)

    q_norm, k_norm = qk_norm(qg_ref[0] * FOX_Q_SCALE), qk_norm(kg_ref[0])
    q_ones = jnp.where((lane >= AUG_S0) & (lane < AUG_S0 + 3), 1.0, 0.0)
    k_ones = jnp.where(((lane >= AUG_T0) & (lane < AUG_T0 + 3))
                       | ((lane >= AUG_ROWMAX0) & (lane < AUG_ROWMAX0 + 3)), 1.0, 0.0)
    cum_terms = _split3(cum * LOG2E)
    for i in range(N_HEADS):
        sl = slice(i * HEAD_DIM, (i + 1) * HEAD_DIM)
        o = 2 * i * HEAD_DIM
        c_hi, c_mid, c_lo = (jnp.broadcast_to(t[:, FF_LANE0 + i:FF_LANE0 + i + 1], (cum.shape[0], LANES))
                             for t in cum_terms)
        q_extra = jnp.where(lane == AUG_T0, c_hi, jnp.where(lane == AUG_T0 + 1, c_mid, jnp.where(
            lane == AUG_T0 + 2, c_lo, q_ones)))
        k_extra = jnp.where(lane == AUG_S0, -c_hi, jnp.where(lane == AUG_S0 + 1, -c_mid, jnp.where(
            lane == AUG_S0 + 2, -c_lo, k_ones)))
        fq_ref[0, :, o:o + HEAD_DIM] = q_norm(fq[:, sl]).astype(BF16)
        fq_ref[0, :, o + HEAD_DIM:o + 2 * HEAD_DIM] = q_extra.astype(BF16)
        fk_ref[0, :, o:o + HEAD_DIM] = k_norm(fk[:, sl]).astype(BF16)
        fk_ref[0, :, o + HEAD_DIM:o + 2 * HEAD_DIM] = k_extra.astype(BF16)

    gqk = proj(4 * w, w)
    r_ref[0, :, 0:w] = heads_map(rq, rope).astype(BF16)
    r_ref[0, :, w:2 * w] = (heads_map(rk, rope) * HEAD_DIM ** -0.5).astype(BF16)
    gqk = jnp.where(lax.broadcasted_iota(jnp.int32, gqk.shape, 1) < _NK, gqk * GLA_DK ** -0.5, gqk)
    gl_ref[0, :, 0:w] = gqk.astype(BF16)
    r_ref[0, :, 3 * w:4 * w] = _silu(proj(3 * w, w)).astype(BF16)
    gl_ref[0, :, 2 * w:3 * w] = _silu(proj(6 * w, w)).astype(BF16)
    r_ref[0, :, 2 * w:3 * w] = proj(2 * w, w).astype(BF16)
    gl_ref[0, :, w:2 * w] = proj(5 * w, w).astype(BF16)
    fv = proj(c0 + 2 * w, w)
    for i in range(N_HEADS):
        o = 2 * i * HEAD_DIM
        fv_ref[0, :, o:o + HEAD_DIM] = fv[:, i * HEAD_DIM:(i + 1) * HEAD_DIM].astype(BF16)
        fv_ref[0, :, o + HEAD_DIM:o + 2 * HEAD_DIM] = jnp.ones((fv.shape[0], HEAD_DIM), BF16)


def _inproj_call(l, x, mod4, g1, wmain, wsmall, wa2, ba, bfox, cos_t, sin_t, qg, kg, tri):
    b, s, d = x.shape
    ts = min(TS_IN, s)
    w = BRANCH_W
    tile = lambda i, j: (i, j, 0)
    return pl.pallas_call(
        _inproj_kernel,
        out_shape=(
            jax.ShapeDtypeStruct((b, s, 4 * w), BF16),
            jax.ShapeDtypeStruct((b, s, 3 * w), BF16),
            jax.ShapeDtypeStruct((b, s, 2 * w), BF16),
            jax.ShapeDtypeStruct((b, s, 2 * w), BF16),
            jax.ShapeDtypeStruct((b, s, 2 * w), BF16),
            jax.ShapeDtypeStruct((b, s, _NK), F32),
        ),
        grid=(b, s // ts),
        in_specs=[
            pl.BlockSpec((1, ts, d), tile),
            pl.BlockSpec((1, 1, 1, mod4.shape[3]), lambda i, j: (l, i, 0, 0)),
            _layer_spec(g1, l),
            _layer_spec(wmain, l),
            _layer_spec(wsmall, l),
            _layer_spec(wa2, l),
            _layer_spec(ba, l),
            _layer_spec(bfox, l),
            pl.BlockSpec((ts, HEAD_DIM), lambda i, j: (j, 0)),
            pl.BlockSpec((ts, HEAD_DIM), lambda i, j: (j, 0)),
            _layer_spec(qg, l),
            _layer_spec(kg, l),
            _const_spec(tri),
        ],
        out_specs=(
            pl.BlockSpec((1, ts, 4 * w), tile),
            pl.BlockSpec((1, ts, 3 * w), tile),
            pl.BlockSpec((1, ts, 2 * w), tile),
            pl.BlockSpec((1, ts, 2 * w), tile),
            pl.BlockSpec((1, ts, 2 * w), tile),
            pl.BlockSpec((1, ts, _NK), tile),
        ),
        scratch_shapes=[pltpu.VMEM((1, LANES), F32)],
        compiler_params=_params("arbitrary", "arbitrary"),
        name="in_proj",
    )(x, mod4, g1, wmain, wsmall, wa2, ba, bfox, cos_t, sin_t, qg, kg, tri)


def _mixer_kernel(r_ref, gl_ref, la_ref, mret_ref, qw_ref, kw_ref, gts_ref, btri_ref, bones_ref,
                  rg_ref, gg_ref, ret_ref, gla_ref, rstate_ref, gstate_ref):
    w = BRANCH_W
    ts = r_ref.shape[1]

    @pl.when(pl.program_id(1) == 0)
    def _():
        rstate_ref[...] = jnp.zeros_like(rstate_ref)
        gstate_ref[...] = jnp.zeros_like(gstate_ref)

    def retention_head(h):
        sl = slice(h * HEAD_DIM, (h + 1) * HEAD_DIM)
        q = r_ref[0, :, sl]
        k = r_ref[0, :, w + h * HEAD_DIM:w + (h + 1) * HEAD_DIM]
        v = r_ref[0, :, 2 * w + h * HEAD_DIM:2 * w + (h + 1) * HEAD_DIM]
        gate = r_ref[0, :, 3 * w + h * HEAD_DIM:3 * w + (h + 1) * HEAD_DIM].astype(F32)
        state = rstate_ref[h]
        p = (_dot_nt(q, k) * mret_ref[h]).astype(BF16)
        o = _dot(p, v) + _dot(q, state.astype(BF16)) * qw_ref[h]
        kk = (k.astype(F32) * kw_ref[h]).astype(BF16)
        rstate_ref[h] = gts_ref[h] * state + _dot_tn(kk, v)
        mu = jnp.mean(o, axis=-1, keepdims=True)
        oc = o - mu
        var = jnp.mean(oc * oc, axis=-1, keepdims=True)
        on = oc * lax.rsqrt(var + EPS) * rg_ref[0, :, sl]
        ret_ref[0, :, sl] = (gate * on).astype(BF16)

    la = la_ref[0]
    b_cum = _split3_dot(btri_ref[...], la)
    b_end = _split3_dot(bones_ref[...], la)
    retention_head(0)
    kdec = jnp.exp(b_end - b_cum)
    a_all = jnp.exp(b_end)
    kk_all = (gl_ref[0, :, _NK:2 * _NK].astype(F32) * kdec).astype(BF16)
    q_all = gl_ref[0, :, 0:_NK]
    lane = lax.broadcasted_iota(jnp.int32, (ts, LANES), 1)
    chunks = [slice(c * CHUNK, (c + 1) * CHUNK) for c in range(ts // CHUNK)]
    pair_lanes = [slice((h // 2) * LANES, (h // 2 + 1) * LANES) for h in range(N_HEADS)]
    kvs = []
    for h in range(N_HEADS):
        kk = kk_all[:, pair_lanes[h]]
        v = gl_ref[0, :, w + h * HEAD_DIM:w + (h + 1) * HEAD_DIM]
        kvs.append([_dot_tn(v[rows], kk[rows]) for rows in chunks])
    retention_head(1)
    states = []
    for h in range(N_HEADS):
        state = gstate_ref[h]
        per_chunk = []
        for c, rows in enumerate(chunks):
            state = a_all[c * CHUNK:c * CHUNK + 1, pair_lanes[h]] * state + kvs[h][c]
            per_chunk.append(state.astype(BF16))
        gstate_ref[h] = state
        states.append(per_chunk)
    retention_head(2)
    for h in range(N_HEADS):
        if h == N_HEADS // 2:
            retention_head(3)
        own = (lane >= (h % 2) * GLA_DK) & (lane < (h % 2 + 1) * GLA_DK)
        qm = jnp.where(own, q_all[:, pair_lanes[h]], jnp.zeros((), BF16))
        gate = gl_ref[0, :, 2 * w + h * HEAD_DIM:2 * w + (h + 1) * HEAD_DIM].astype(F32)
        o = jnp.concatenate([_dot_nt(qm[rows], states[h][c]) for c, rows in enumerate(chunks)], axis=0)
        ms = jnp.mean(o * o, axis=-1, keepdims=True)
        on = o * lax.rsqrt(ms + EPS) * gg_ref[0]
        gla_ref[0, :, h * HEAD_DIM:(h + 1) * HEAD_DIM] = (gate * on).astype(BF16)


def _mixer_call(l, r, gl, la, mret, qw, kw, gts, btri, bones, rg, gg):
    b, s, _ = r.shape
    ts = mret.shape[1]
    w = BRANCH_W
    tile = lambda i, j: (i, j, 0)
    return pl.pallas_call(
        _mixer_kernel,
        out_shape=(jax.ShapeDtypeStruct((b, s, w), BF16), jax.ShapeDtypeStruct((b, s, w), BF16)),
        grid=(b, s // ts),
        in_specs=[
            pl.BlockSpec((1, ts, 4 * w), tile),
            pl.BlockSpec((1, ts, 3 * w), tile),
            pl.BlockSpec((1, ts, _NK), tile),
            _const_spec(mret),
            _const_spec(qw),
            _const_spec(kw),
            _const_spec(gts),
            _const_spec(btri),
            _const_spec(bones),
            _layer_spec(rg, l),
            _layer_spec(gg, l),
        ],
        out_specs=(pl.BlockSpec((1, ts, w), tile), pl.BlockSpec((1, ts, w), tile)),
        scratch_shapes=[pltpu.VMEM((N_HEADS, HEAD_DIM, HEAD_DIM), F32),
                        pltpu.VMEM((N_HEADS, HEAD_DIM, LANES), F32)],
        compiler_params=_params("arbitrary", "arbitrary"),
        name="ret_gla_mixer",
    )(r, gl, la, mret, qw, kw, gts, btri, bones, rg, gg)


def _fox_kernel(q_ref, k_ref, v_ref, qg_ref, kg_ref, o_ref, mpart_ref, q2_ref, acc_ref):
    tq = q_ref.shape[1]
    qi = pl.program_id(1)
    hw = 2 * HEAD_DIM
    hs = [slice(h * hw, (h + 1) * hw) for h in range(N_HEADS)]
    half = tq // 2
    whole, top, bottom = slice(0, tq), slice(0, half), slice(half, tq)
    top_mask = (lax.broadcasted_iota(jnp.int32, (half, half), 0)
                >= lax.broadcasted_iota(jnp.int32, (half, half), 1))
    bottom_mask = (lax.broadcasted_iota(jnp.int32, (half, tq), 1)
                   <= lax.broadcasted_iota(jnp.int32, (half, tq), 0) + half)

    def scores(lhs, keys, mask):
        out = [_dot_nt(lhs(h), k_ref[0, keys, hs[h]]) for h in range(N_HEADS)]
        if mask is not None:
            out = [jnp.where(mask, s, -jnp.inf) for s in out]
        return out

    def pass1(qrows, start, width, mask):
        keys = pl.ds(pl.multiple_of(start, width), width)
        for h, s in enumerate(scores(lambda h: q_ref[0, qrows, hs[h]], keys, mask)):
            m = mpart_ref[h, qrows]
            for t in range(width // LANES):
                m = jnp.maximum(m, s[:, t * LANES:(t + 1) * LANES])
            mpart_ref[h, qrows] = m

    def pass2(qrows, start, width, mask):
        keys = pl.ds(pl.multiple_of(start, width), width)
        ps = [jnp.exp2(s).astype(BF16) for s in scores(lambda h: q2_ref[h, qrows], keys, mask)]
        for h in range(N_HEADS):
            acc_ref[h, qrows] += _dot(ps[h], v_ref[0, keys, hs[h]])

    def loop(fn):
        def body(jb, carry):
            fn(whole, jb * tq, tq, None)
            return carry
        lax.fori_loop(0, qi, body, 0)
        fn(top, qi * tq, half, top_mask)
        fn(bottom, qi * tq, tq, bottom_mask)

    gain = (jnp.max(jnp.abs(qg_ref[0]), axis=-1, keepdims=True)
            * jnp.max(jnp.abs(kg_ref[0]), axis=-1, keepdims=True))
    bound = gain * (FOX_Q_SCALE * HEAD_DIM * NORM_SLACK)
    bound_is_safe = jnp.max(bound) < BOUND_LIMIT

    @pl.when(bound_is_safe)
    def _():
        mpart_ref[...] = jnp.broadcast_to(bound.reshape(1, 1, 1), mpart_ref.shape)

    @pl.when(jnp.logical_not(bound_is_safe))
    def _():
        mpart_ref[...] = jnp.full(mpart_ref.shape, -jnp.inf, F32)
        loop(pass1)

    lane = lax.broadcasted_iota(jnp.int32, (tq, LANES), 1)
    for h in range(N_HEADS):
        m = jnp.broadcast_to(jnp.max(mpart_ref[h], axis=-1, keepdims=True), (tq, LANES))
        m_hi, m_mid, m_lo = _split3(m)
        extra = q_ref[0, :, h * hw + HEAD_DIM:(h + 1) * hw].astype(F32)
        extra = jnp.where(lane == AUG_ROWMAX0, -m_hi, jnp.where(
            lane == AUG_ROWMAX0 + 1, -m_mid, jnp.where(lane == AUG_ROWMAX0 + 2, -m_lo, extra)))
        q2_ref[h, :, 0:HEAD_DIM] = q_ref[0, :, h * hw:h * hw + HEAD_DIM]
        q2_ref[h, :, HEAD_DIM:hw] = extra.astype(BF16)
    acc_ref[...] = jnp.zeros_like(acc_ref)
    loop(pass2)
    for h in range(N_HEADS):
        acc = acc_ref[h]
        o_ref[0, :, h * HEAD_DIM:(h + 1) * HEAD_DIM] = (acc[:, 0:HEAD_DIM] / acc[:, HEAD_DIM:hw]).astype(BF16)


def _fox_call(l, fq, fk, fv, qg, kg, tq):
    b, s, wide = fq.shape
    assert s % tq == 0
    return pl.pallas_call(
        _fox_kernel,
        out_shape=jax.ShapeDtypeStruct((b, s, BRANCH_W), BF16),
        grid=(b, s // tq),
        in_specs=[
            pl.BlockSpec((1, tq, wide), lambda i, j: (i, j, 0)),
            pl.BlockSpec((1, s, wide), lambda i, j: (i, 0, 0)),
            pl.BlockSpec((1, s, wide), lambda i, j: (i, 0, 0)),
            _layer_spec(qg, l),
            _layer_spec(kg, l),
        ],
        out_specs=pl.BlockSpec((1, tq, BRANCH_W), lambda i, j: (i, j, 0)),
        scratch_shapes=[pltpu.VMEM((N_HEADS, tq, LANES), F32),
                        pltpu.VMEM((N_HEADS, tq, 2 * HEAD_DIM), BF16),
                        pltpu.VMEM((N_HEADS, tq, 2 * HEAD_DIM), F32)],
        compiler_params=_params("arbitrary", "arbitrary"),
        name="fox_attention",
    )(fq, fk, fv, qg, kg)


def _outproj_kernel(x_ref, mod_ref, g1_ref, ret_ref, gla_ref, fox_ref, wbr_ref, wmg_ref, bmg_ref, wo_ref,
                    o_ref):
    d = x_ref.shape[2]
    x = x_ref[0]
    mod = mod_ref[0, 0]
    hb = _norm_modulate(x, g1_ref[0], mod[:, 0:d], mod[:, d:2 * d]).astype(BF16)
    mixed = None
    for n, br_ref in enumerate((ret_ref, gla_ref, fox_ref)):
        y = _dot(br_ref[0], wbr_ref[0, n])
        gate = _sigmoid(_dot(hb, wmg_ref[0, :, n * d:(n + 1) * d]) + bmg_ref[0, :, n * d:(n + 1) * d])
        mixed = gate * y if mixed is None else mixed + gate * y
    out = _dot(mixed.astype(BF16), wo_ref[0])
    o_ref[0] = x + mod[:, 2 * d:3 * d] * out


def _outproj_call(l, x, mod4, g1, ret, gla, fox, wbr, wmg, bmg, wo):
    b, s, d = x.shape
    ts = min(TS_OUT, s)
    tile = lambda i, j: (i, j, 0)
    return pl.pallas_call(
        _outproj_kernel,
        out_shape=jax.ShapeDtypeStruct((b, s, d), F32),
        grid=(b, s // ts),
        in_specs=[
            pl.BlockSpec((1, ts, d), tile),
            pl.BlockSpec((1, 1, 1, mod4.shape[3]), lambda i, j: (l, i, 0, 0)),
            _layer_spec(g1, l),
            pl.BlockSpec((1, ts, BRANCH_W), tile),
            pl.BlockSpec((1, ts, BRANCH_W), tile),
            pl.BlockSpec((1, ts, BRANCH_W), tile),
            _layer_spec(wbr, l),
            _layer_spec(wmg, l),
            _layer_spec(bmg, l),
            _layer_spec(wo, l),
        ],
        out_specs=pl.BlockSpec((1, ts, d), tile),
        compiler_params=_params("arbitrary", "arbitrary"),
        name="out_proj",
    )(x, mod4, g1, ret, gla, fox, wbr, wmg, bmg, wo)


def _ffn_kernel(x_ref, mod_ref, g2_ref, wup_ref, wconv_ref, bconv_ref, wdown_ref, o_ref, tail_ref):
    d = x_ref.shape[2]
    ts = x_ref.shape[1]
    ff = wdown_ref.shape[1]

    @pl.when(pl.program_id(1) == 0)
    def _():
        tail_ref[...] = jnp.zeros_like(tail_ref)

    x = x_ref[0]
    mod = mod_ref[0, 0]
    hb = _norm_modulate(x, g2_ref[0], mod[:, 3 * d:4 * d], mod[:, 4 * d:5 * d]).astype(BF16)
    u = _dot(hb, wup_ref[0, :, 0:ff])
    g = _dot(hb, wup_ref[0, :, ff:2 * ff])
    row = lax.broadcasted_iota(jnp.int32, u.shape, 0)
    prev1 = tail_ref[CONV_W - 2:CONV_W - 1, :]
    prev2 = tail_ref[CONV_W - 3:CONV_W - 2, :]
    u1 = jnp.where(row == 0, prev1, pltpu.roll(u, 1, axis=0))
    u2 = jnp.where(row == 0, prev2, jnp.where(row == 1, prev1, pltpu.roll(u, 2, axis=0)))
    tail_ref[...] = u[ts - (CONV_W - 1):ts, :]
    wc = wconv_ref[0]
    conv = bconv_ref[0] + wc[0:1, :] * u2 + wc[1:2, :] * u1 + wc[2:3, :] * u
    act = (_silu(conv) * g).astype(BF16)
    y = _dot(act, wdown_ref[0])
    o_ref[0] = x + mod[:, 5 * d:6 * d] * y


def _ffn_call(l, x, mod4, g2, wup, wconv, bconv, wdown):
    b, s, d = x.shape
    ts = min(TS_FFN, s)
    ff = wdown.shape[1]
    tile = lambda i, j: (i, j, 0)
    return pl.pallas_call(
        _ffn_kernel,
        out_shape=jax.ShapeDtypeStruct((b, s, d), F32),
        grid=(b, s // ts),
        in_specs=[
            pl.BlockSpec((1, ts, d), tile),
            pl.BlockSpec((1, 1, 1, mod4.shape[3]), lambda i, j: (l, i, 0, 0)),
            _layer_spec(g2, l),
            _layer_spec(wup, l),
            _layer_spec(wconv, l),
            _layer_spec(bconv, l),
            _layer_spec(wdown, l),
        ],
        out_specs=pl.BlockSpec((1, ts, d), tile),
        scratch_shapes=[pltpu.VMEM((CONV_W - 1, ff), F32)],
        compiler_params=_params("arbitrary", "arbitrary"),
        name="conv_ffn",
    )(x, mod4, g2, wup, wconv, bconv, wdown)


def _rope_tables(s):
    half = HEAD_DIM // 2
    pos = jnp.arange(s, dtype=F32)
    inv_freq = ROPE_BASE ** (-jnp.arange(half, dtype=F32) / half)
    ang = pos[:, None] * inv_freq[None, :]
    cos, sin = jnp.cos(ang), jnp.sin(ang)
    return jnp.concatenate([cos, cos], axis=1), jnp.concatenate([-sin, sin], axis=1)


def _retention_tables(ts):
    log_g = jnp.log1p(-jnp.exp2(-5.0 - jnp.arange(N_HEADS, dtype=F32)))[:, None, None]
    t = jnp.arange(ts, dtype=F32)
    diff = t[:, None] - t[None, :]
    same = (jnp.arange(ts)[:, None] // CHUNK) == (jnp.arange(ts)[None, :] // CHUNK)
    expo = jnp.where(same, jnp.abs(diff), diff)[None]
    mask = jnp.where((same | (diff > 0))[None], jnp.exp(jnp.where(expo >= 0, expo, 0.0) * log_g), 0.0)
    ones = jnp.ones((1, 1, HEAD_DIM), F32)
    qw = jnp.exp((t + 1.0)[None, :, None] * log_g) * ones
    kw = jnp.exp((ts - 1.0 - t)[None, :, None] * log_g) * ones
    gts = jnp.exp(ts * log_g) * ones
    return mask, qw, kw, gts


def _chunk_matrices(ts):
    i = jnp.arange(ts)
    same = (i[:, None] // CHUNK) == (i[None, :] // CHUNK)
    btri = (same & (i[:, None] >= i[None, :])).astype(BF16)
    return btri, same.astype(BF16)


def kernel(x, c, norm1_g, norm2_g, w_ada, b_ada, w_in, w_gla_a2, b_gla_a, b_fox_f, ret_norm_g, gla_norm_g,
           q_norm_g, k_norm_g, w_br, w_mg, b_mg, w_o, w_up, w_conv, b_conv, w_down):
    b, s, d = x.shape
    depth = w_ada.shape[0]
    row = lambda a: a[:, None, :]

    mod4 = _ada_call(c, w_ada, b_ada)[:, :, None, :]
    wmain, wsmall = _win_prep_call(w_in.astype(BF16))
    wa2 = jnp.pad(w_gla_a2, ((0, 0), (0, LANES - GLA_LOWRANK), (0, 0))).astype(BF16)
    bfox = jnp.pad(b_fox_f, ((0, 0), (FF_LANE0, LANES - FF_LANE0 - N_HEADS)))
    wbr, wmg, wo, wup, wdown = (a.astype(BF16) for a in (w_br, w_mg, w_o, w_up, w_down))
    g1, g2, ba, bfox, rg, gg, qg, kg, bmg, bconv = (
        row(a) for a in (norm1_g, norm2_g, b_gla_a, bfox, ret_norm_g, gla_norm_g, q_norm_g, k_norm_g,
                         b_mg, b_conv))

    ts_in = min(TS_IN, s)
    ts_mix = min(TS_MIX, s)
    i = jnp.arange(ts_in)
    tri = (i[:, None] >= i[None, :]).astype(BF16)
    cos_t, sin_t = _rope_tables(s)
    mret, qw, kw, gts = _retention_tables(ts_mix)
    btri, bones = _chunk_matrices(ts_mix)

    for l in range(depth):
        r, gl, fq, fk, fv, la = _inproj_call(
            l, x, mod4, g1, wmain, wsmall, wa2, ba, bfox, cos_t, sin_t, qg, kg, tri)
        ret, gla_o = _mixer_call(l, r, gl, la, mret, qw, kw, gts, btri, bones, rg, gg)
        fox = _fox_call(l, fq, fk, fv, qg, kg, min(TQ_FOX, s))
        x = _outproj_call(l, x, mod4, g1, ret, gla_o, fox, wbr, wmg, bmg, wo)
        x = _ffn_call(l, x, mod4, g2, wup, w_conv, bconv, wdown)
    return x
```

```python
import jax
import jax.numpy as jnp
from jax import lax
from jax.experimental import pallas as pl
from jax.experimental.pallas import tpu as pltpu

F32 = jnp.float32
BF16 = jnp.bfloat16

CHUNK = 64
N_HEADS = 4
HEAD_DIM = 128
BRANCH_W = N_HEADS * HEAD_DIM
GLA_DK = HEAD_DIM // 2
GLA_LOWRANK = 16
GLA_TAU = 16.0
CONV_W = 3
ROPE_BASE = 10000.0
EPS = 1e-6

LANES = 128
FF_LANE0 = GLA_LOWRANK
AUG_T0 = 0
AUG_S0 = 3
AUG_ROWMAX0 = 6
LOG2E = 1.4426950408889634
FOX_Q_SCALE = HEAD_DIM ** -0.5 * LOG2E
NORM_SLACK = 1.02
BOUND_LIMIT = 40.0

VMEM_LIMIT = 56 * 1024 * 1024

TS_IN = 512
TS_MIX = 256
TQ_FOX = 512
TS_OUT = 512
TS_FFN = 512
TR_PREP = 256
_NK = N_HEADS * GLA_DK
O_GLR = 4 * BRANCH_W + 2 * _NK + BRANCH_W
O_GG = O_GLR + GLA_LOWRANK
O_FF = O_GG + 4 * BRANCH_W
W_MAIN = O_GLR + (O_FF - O_GG)


def _dot(a, b):
    return jnp.dot(a, b, preferred_element_type=F32)


def _dot_nt(a, b):
    return lax.dot_general(a, b, (((1,), (1,)), ((), ())), preferred_element_type=F32)


def _dot_tn(a, b):
    return lax.dot_general(a, b, (((0,), (0,)), ((), ())), preferred_element_type=F32)


def _split3(x):
    x_hi = x.astype(BF16).astype(F32)
    r1 = x - x_hi
    x_mid = r1.astype(BF16).astype(F32)
    x_lo = (r1 - x_mid).astype(BF16).astype(F32)
    return x_hi, x_mid, x_lo


def _split3_dot(m_bf16, x):
    x_hi, x_mid, x_lo = (t.astype(BF16) for t in _split3(x))
    return _dot(m_bf16, x_hi) + _dot(m_bf16, x_mid) + _dot(m_bf16, x_lo)


def _log_sigmoid(x):
    return jnp.minimum(x, 0.0) - jnp.log1p(jnp.exp(-jnp.abs(x)))


def _sigmoid(x):
    return 1.0 / (1.0 + jnp.exp(-x))


def _silu(x):
    return x * _sigmoid(x)


def _norm_modulate(x, g, shift, scale):
    ms = jnp.mean(x * x, axis=-1, keepdims=True)
    return x * lax.rsqrt(ms + EPS) * (g * (1.0 + scale)) + shift


def _params(*sem):
    return pltpu.CompilerParams(dimension_semantics=sem, vmem_limit_bytes=VMEM_LIMIT)


def _layer_spec(arr, l):
    zeros = (0,) * (arr.ndim - 1)
    return pl.BlockSpec((1,) + arr.shape[1:], lambda *_: (l,) + zeros, pipeline_mode=pl.Buffered(1))


def _const_spec(arr):
    zeros = (0,) * arr.ndim
    return pl.BlockSpec(arr.shape, lambda *_: zeros, pipeline_mode=pl.Buffered(1))


def _ada_kernel(c_ref, w_ref, b_ref, o_ref):
    c_act = _silu(c_ref[...]).astype(BF16)
    o_ref[0] = _dot(c_act, w_ref[0].astype(BF16)) + b_ref[0]


def _ada_call(c, w_ada, b_ada):
    depth, d, n = w_ada.shape
    b = c.shape[0]
    tn = n // 4
    return pl.pallas_call(
        _ada_kernel,
        out_shape=jax.ShapeDtypeStruct((depth, b, n), F32),
        grid=(depth, n // tn),
        in_specs=[
            pl.BlockSpec((b, d), lambda l, j: (0, 0)),
            pl.BlockSpec((1, d, tn), lambda l, j: (l, 0, j)),
            pl.BlockSpec((1, 1, tn), lambda l, j: (l, 0, j)),
        ],
        out_specs=pl.BlockSpec((1, b, tn), lambda l, j: (l, 0, j)),
        compiler_params=_params("arbitrary", "arbitrary"),
        name="ada_mod",
    )(c, w_ada, b_ada.reshape(depth, 1, n))


def _win_prep_kernel(w_ref, main_ref, small_ref):
    tr = w_ref.shape[1]
    main_ref[0, :, 0:O_GLR] = w_ref[0, :, 0:O_GLR].astype(BF16)
    main_ref[0, :, O_GLR:W_MAIN] = w_ref[0, :, O_GG:O_FF].astype(BF16)
    small = jnp.concatenate(
        [w_ref[0, :, O_GLR:O_GG], w_ref[0, :, O_FF:O_FF + N_HEADS],
         jnp.zeros((tr, LANES - GLA_LOWRANK - N_HEADS), w_ref.dtype)], axis=1)
    small_ref[0] = small.astype(BF16)


def _win_prep_call(w_in):
    depth, d, n = w_in.shape
    tr = TR_PREP
    tile = lambda l, i: (l, i, 0)
    return pl.pallas_call(
        _win_prep_kernel,
        out_shape=(jax.ShapeDtypeStruct((depth, d, W_MAIN), BF16),
                   jax.ShapeDtypeStruct((depth, d, LANES), BF16)),
        grid=(depth, d // tr),
        in_specs=[pl.BlockSpec((1, tr, n), tile)],
        out_specs=(pl.BlockSpec((1, tr, W_MAIN), tile), pl.BlockSpec((1, tr, LANES), tile)),
        compiler_params=_params("arbitrary", "arbitrary"),
        name="win_prep",
    )(w_in)


def _inproj_kernel(x_ref, mod_ref, g1_ref, wmain_ref, wsmall_ref, wa2_ref, ba_ref, bfox_ref,
                   cos_ref, sin_ref, qg_ref, kg_ref, tri_ref,
                   r_ref, gl_ref, fq_ref, fk_ref, fv_ref, la_ref, carry_ref):
    d = x_ref.shape[2]
    w = BRANCH_W

    @pl.when(pl.program_id(1) == 0)
    def _():
        carry_ref[...] = jnp.zeros_like(carry_ref)

    x = x_ref[0]
    mod = mod_ref[0, 0]
    h = _norm_modulate(x, g1_ref[0], mod[:, 0:d], mod[:, d:2 * d])
    hb = h.astype(BF16)
    cos = cos_ref[...]
    sin = sin_ref[...]

    def heads_map(acc, fn):
        return jnp.concatenate(
            [fn(acc[:, i * HEAD_DIM:(i + 1) * HEAD_DIM]) for i in range(N_HEADS)], axis=1)

    def rope(t):
        return t * cos + pltpu.roll(t, HEAD_DIM // 2, axis=1) * sin

    def qk_norm(g):
        def fn(t):
            ms = jnp.mean(t * t, axis=-1, keepdims=True)
            return t * lax.rsqrt(ms + EPS) * g
        return fn

    def proj(c0, n):
        return _dot(hb, wmain_ref[0, :, c0:c0 + n])

    small = _dot(hb, wsmall_ref[0])
    c0 = 7 * w
    fq, fk = proj(c0, w), proj(c0 + w, w)
    la_pre = _dot(small.astype(BF16), wa2_ref[0]) + ba_ref[0]
    la_ref[0] = _log_sigmoid(la_pre) * (1.0 / GLA_TAU)
    lane = lax.broadcasted_iota(jnp.int32, small.shape, 1)
    is_ff = (lane >= FF_LANE0) & (lane < FF_LANE0 + N_HEADS)
    log_f = jnp.where(is_ff, _log_sigmoid(small + bfox_ref[0]), 0.0)
    cum = _split3_dot(tri_ref[...], log_f) + carry_ref[...]
    carry_ref[...] = cum[cum.shape[0] - 1:cum.shape[0], :]
    rq, rk = proj(0, w), proj(w, w)

    q_norm, k_norm = qk_norm(qg_ref[0] * FOX_Q_SCALE), qk_norm(kg_ref[0])
    q_ones = jnp.where((lane >= AUG_S0) & (lane < AUG_S0 + 3), 1.0, 0.0)
    k_ones = jnp.where(((lane >= AUG_T0) & (lane < AUG_T0 + 3))
                       | ((lane >= AUG_ROWMAX0) & (lane < AUG_ROWMAX0 + 3)), 1.0, 0.0)
    cum_terms = _split3(cum * LOG2E)
    for i in range(N_HEADS):
        sl = slice(i * HEAD_DIM, (i + 1) * HEAD_DIM)
        o = 2 * i * HEAD_DIM
        c_hi, c_mid, c_lo = (jnp.broadcast_to(t[:, FF_LANE0 + i:FF_LANE0 + i + 1], (cum.shape[0], LANES))
                             for t in cum_terms)
        q_extra = jnp.where(lane == AUG_T0, c_hi, jnp.where(lane == AUG_T0 + 1, c_mid, jnp.where(
            lane == AUG_T0 + 2, c_lo, q_ones)))
        k_extra = jnp.where(lane == AUG_S0, -c_hi, jnp.where(lane == AUG_S0 + 1, -c_mid, jnp.where(
            lane == AUG_S0 + 2, -c_lo, k_ones)))
        fq_ref[0, :, o:o + HEAD_DIM] = q_norm(fq[:, sl]).astype(BF16)
        fq_ref[0, :, o + HEAD_DIM:o + 2 * HEAD_DIM] = q_extra.astype(BF16)
        fk_ref[0, :, o:o + HEAD_DIM] = k_norm(fk[:, sl]).astype(BF16)
        fk_ref[0, :, o + HEAD_DIM:o + 2 * HEAD_DIM] = k_extra.astype(BF16)

    gqk = proj(4 * w, w)
    r_ref[0, :, 0:w] = heads_map(rq, rope).astype(BF16)
    r_ref[0, :, w:2 * w] = (heads_map(rk, rope) * HEAD_DIM ** -0.5).astype(BF16)
    gqk = jnp.where(lax.broadcasted_iota(jnp.int32, gqk.shape, 1) < _NK, gqk * GLA_DK ** -0.5, gqk)
    gl_ref[0, :, 0:w] = gqk.astype(BF16)
    r_ref[0, :, 3 * w:4 * w] = _silu(proj(3 * w, w)).astype(BF16)
    gl_ref[0, :, 2 * w:3 * w] = _silu(proj(6 * w, w)).astype(BF16)
    r_ref[0, :, 2 * w:3 * w] = proj(2 * w, w).astype(BF16)
    gl_ref[0, :, w:2 * w] = proj(5 * w, w).astype(BF16)
    fv = proj(c0 + 2 * w, w)
    for i in range(N_HEADS):
        o = 2 * i * HEAD_DIM
        fv_ref[0, :, o:o + HEAD_DIM] = fv[:, i * HEAD_DIM:(i + 1) * HEAD_DIM].astype(BF16)
        fv_ref[0, :, o + HEAD_DIM:o + 2 * HEAD_DIM] = jnp.ones((fv.shape[0], HEAD_DIM), BF16)


def _inproj_call(l, x, mod4, g1, wmain, wsmall, wa2, ba, bfox, cos_t, sin_t, qg, kg, tri):
    b, s, d = x.shape
    ts = min(TS_IN, s)
    w = BRANCH_W
    tile = lambda i, j: (i, j, 0)
    return pl.pallas_call(
        _inproj_kernel,
        out_shape=(
            jax.ShapeDtypeStruct((b, s, 4 * w), BF16),
            jax.ShapeDtypeStruct((b, s, 3 * w), BF16),
            jax.ShapeDtypeStruct((b, s, 2 * w), BF16),
            jax.ShapeDtypeStruct((b, s, 2 * w), BF16),
            jax.ShapeDtypeStruct((b, s, 2 * w), BF16),
            jax.ShapeDtypeStruct((b, s, _NK), F32),
        ),
        grid=(b, s // ts),
        in_specs=[
            pl.BlockSpec((1, ts, d), tile),
            pl.BlockSpec((1, 1, 1, mod4.shape[3]), lambda i, j: (l, i, 0, 0)),
            _layer_spec(g1, l),
            _layer_spec(wmain, l),
            _layer_spec(wsmall, l),
            _layer_spec(wa2, l),
            _layer_spec(ba, l),
            _layer_spec(bfox, l),
            pl.BlockSpec((ts, HEAD_DIM), lambda i, j: (j, 0)),
            pl.BlockSpec((ts, HEAD_DIM), lambda i, j: (j, 0)),
            _layer_spec(qg, l),
            _layer_spec(kg, l),
            _const_spec(tri),
        ],
        out_specs=(
            pl.BlockSpec((1, ts, 4 * w), tile),
            pl.BlockSpec((1, ts, 3 * w), tile),
            pl.BlockSpec((1, ts, 2 * w), tile),
            pl.BlockSpec((1, ts, 2 * w), tile),
            pl.BlockSpec((1, ts, 2 * w), tile),
            pl.BlockSpec((1, ts, _NK), tile),
        ),
        scratch_shapes=[pltpu.VMEM((1, LANES), F32)],
        compiler_params=_params("arbitrary", "arbitrary"),
        name="in_proj",
    )(x, mod4, g1, wmain, wsmall, wa2, ba, bfox, cos_t, sin_t, qg, kg, tri)


def _mixer_kernel(r_ref, gl_ref, la_ref, mret_ref, qw_ref, kw_ref, gts_ref, btri_ref, bones_ref,
                  rg_ref, gg_ref, ret_ref, gla_ref, rstate_ref, gstate_ref):
    w = BRANCH_W
    ts = r_ref.shape[1]

    @pl.when(pl.program_id(1) == 0)
    def _():
        rstate_ref[...] = jnp.zeros_like(rstate_ref)
        gstate_ref[...] = jnp.zeros_like(gstate_ref)

    def retention_head(h):
        sl = slice(h * HEAD_DIM, (h + 1) * HEAD_DIM)
        q = r_ref[0, :, sl]
        k = r_ref[0, :, w + h * HEAD_DIM:w + (h + 1) * HEAD_DIM]
        v = r_ref[0, :, 2 * w + h * HEAD_DIM:2 * w + (h + 1) * HEAD_DIM]
        gate = r_ref[0, :, 3 * w + h * HEAD_DIM:3 * w + (h + 1) * HEAD_DIM].astype(F32)
        state = rstate_ref[h]
        p = (_dot_nt(q, k) * mret_ref[h]).astype(BF16)
        o = _dot(p, v) + _dot(q, state.astype(BF16)) * qw_ref[h]
        kk = (k.astype(F32) * kw_ref[h]).astype(BF16)
        rstate_ref[h] = gts_ref[h] * state + _dot_tn(kk, v)
        mu = jnp.mean(o, axis=-1, keepdims=True)
        oc = o - mu
        var = jnp.mean(oc * oc, axis=-1, keepdims=True)
        on = oc * lax.rsqrt(var + EPS) * rg_ref[0, :, sl]
        ret_ref[0, :, sl] = (gate * on).astype(BF16)

    la = la_ref[0]
    b_cum = _split3_dot(btri_ref[...], la)
    b_end = _split3_dot(bones_ref[...], la)
    retention_head(0)
    kdec = jnp.exp(b_end - b_cum)
    a_all = jnp.exp(b_end)
    kk_all = (gl_ref[0, :, _NK:2 * _NK].astype(F32) * kdec).astype(BF16)
    q_all = gl_ref[0, :, 0:_NK]
    lane = lax.broadcasted_iota(jnp.int32, (ts, LANES), 1)
    chunks = [slice(c * CHUNK, (c + 1) * CHUNK) for c in range(ts // CHUNK)]
    pair_lanes = [slice((h // 2) * LANES, (h // 2 + 1) * LANES) for h in range(N_HEADS)]
    kvs = []
    for h in range(N_HEADS):
        kk = kk_all[:, pair_lanes[h]]
        v = gl_ref[0, :, w + h * HEAD_DIM:w + (h + 1) * HEAD_DIM]
        kvs.append([_dot_tn(v[rows], kk[rows]) for rows in chunks])
    retention_head(1)
    states = []
    for h in range(N_HEADS):
        state = gstate_ref[h]
        per_chunk = []
        for c, rows in enumerate(chunks):
            state = a_all[c * CHUNK:c * CHUNK + 1, pair_lanes[h]] * state + kvs[h][c]
            per_chunk.append(state.astype(BF16))
        gstate_ref[h] = state
        states.append(per_chunk)
    retention_head(2)
    for h in range(N_HEADS):
        if h == N_HEADS // 2:
            retention_head(3)
        own = (lane >= (h % 2) * GLA_DK) & (lane < (h % 2 + 1) * GLA_DK)
        qm = jnp.where(own, q_all[:, pair_lanes[h]], jnp.zeros((), BF16))
        gate = gl_ref[0, :, 2 * w + h * HEAD_DIM:2 * w + (h + 1) * HEAD_DIM].astype(F32)
        o = jnp.concatenate([_dot_nt(qm[rows], states[h][c]) for c, rows in enumerate(chunks)], axis=0)
        ms = jnp.mean(o * o, axis=-1, keepdims=True)
        on = o * lax.rsqrt(ms + EPS) * gg_ref[0]
        gla_ref[0, :, h * HEAD_DIM:(h + 1) * HEAD_DIM] = (gate * on).astype(BF16)


def _mixer_call(l, r, gl, la, mret, qw, kw, gts, btri, bones, rg, gg):
    b, s, _ = r.shape
    ts = mret.shape[1]
    w = BRANCH_W
    tile = lambda i, j: (i, j, 0)
    return pl.pallas_call(
        _mixer_kernel,
        out_shape=(jax.ShapeDtypeStruct((b, s, w), BF16), jax.ShapeDtypeStruct((b, s, w), BF16)),
        grid=(b, s // ts),
        in_specs=[
            pl.BlockSpec((1, ts, 4 * w), tile),
            pl.BlockSpec((1, ts, 3 * w), tile),
            pl.BlockSpec((1, ts, _NK), tile),
            _const_spec(mret),
            _const_spec(qw),
            _const_spec(kw),
            _const_spec(gts),
            _const_spec(btri),
            _const_spec(bones),
            _layer_spec(rg, l),
            _layer_spec(gg, l),
        ],
        out_specs=(pl.BlockSpec((1, ts, w), tile), pl.BlockSpec((1, ts, w), tile)),
        scratch_shapes=[pltpu.VMEM((N_HEADS, HEAD_DIM, HEAD_DIM), F32),
                        pltpu.VMEM((N_HEADS, HEAD_DIM, LANES), F32)],
        compiler_params=_params("arbitrary", "arbitrary"),
        name="ret_gla_mixer",
    )(r, gl, la, mret, qw, kw, gts, btri, bones, rg, gg)


def _fox_kernel(q_ref, k_ref, v_ref, qg_ref, kg_ref, o_ref, mpart_ref, q2_ref, acc_ref):
    tq = q_ref.shape[1]
    qi = pl.program_id(1)
    hw = 2 * HEAD_DIM
    hs = [slice(h * hw, (h + 1) * hw) for h in range(N_HEADS)]
    half = tq // 2
    whole, top, bottom = slice(0, tq), slice(0, half), slice(half, tq)
    top_mask = (lax.broadcasted_iota(jnp.int32, (half, half), 0)
                >= lax.broadcasted_iota(jnp.int32, (half, half), 1))
    bottom_mask = (lax.broadcasted_iota(jnp.int32, (half, tq), 1)
                   <= lax.broadcasted_iota(jnp.int32, (half, tq), 0) + half)

    def scores(lhs, keys, mask):
        out = [_dot_nt(lhs(h), k_ref[0, keys, hs[h]]) for h in range(N_HEADS)]
        if mask is not None:
            out = [jnp.where(mask, s, -jnp.inf) for s in out]
        return out

    def pass1(qrows, start, width, mask):
        keys = pl.ds(pl.multiple_of(start, width), width)
        for h, s in enumerate(scores(lambda h: q_ref[0, qrows, hs[h]], keys, mask)):
            m = mpart_ref[h, qrows]
            for t in range(width // LANES):
                m = jnp.maximum(m, s[:, t * LANES:(t + 1) * LANES])
            mpart_ref[h, qrows] = m

    def pass2(qrows, start, width, mask):
        keys = pl.ds(pl.multiple_of(start, width), width)
        ps = [jnp.exp2(s).astype(BF16) for s in scores(lambda h: q2_ref[h, qrows], keys, mask)]
        for h in range(N_HEADS):
            acc_ref[h, qrows] += _dot(ps[h], v_ref[0, keys, hs[h]])

    def loop(fn):
        def body(jb, carry):
            fn(whole, jb * tq, tq, None)
            return carry
        lax.fori_loop(0, qi, body, 0)
        fn(top, qi * tq, half, top_mask)
        fn(bottom, qi * tq, tq, bottom_mask)

    gain = (jnp.max(jnp.abs(qg_ref[0]), axis=-1, keepdims=True)
            * jnp.max(jnp.abs(kg_ref[0]), axis=-1, keepdims=True))
    bound = gain * (FOX_Q_SCALE * HEAD_DIM * NORM_SLACK)
    bound_is_safe = jnp.max(bound) < BOUND_LIMIT

    lane = lax.broadcasted_iota(jnp.int32, (tq, LANES), 1)

    def set_reference(h, m):
        m_hi, m_mid, m_lo = _split3(m)
        extra = q_ref[0, :, h * hw + HEAD_DIM:(h + 1) * hw].astype(F32)
        extra = jnp.where(lane == AUG_ROWMAX0, -m_hi, jnp.where(
            lane == AUG_ROWMAX0 + 1, -m_mid, jnp.where(lane == AUG_ROWMAX0 + 2, -m_lo, extra)))
        q2_ref[h, :, 0:HEAD_DIM] = q_ref[0, :, h * hw:h * hw + HEAD_DIM]
        q2_ref[h, :, HEAD_DIM:hw] = extra.astype(BF16)

    @pl.when(bound_is_safe)
    def _():
        for h in range(N_HEADS):
            set_reference(h, jnp.broadcast_to(bound, (1, LANES)))

    @pl.when(jnp.logical_not(bound_is_safe))
    def _():
        mpart_ref[...] = jnp.full(mpart_ref.shape, -jnp.inf, F32)
        loop(pass1)
        for h in range(N_HEADS):
            set_reference(h, jnp.broadcast_to(jnp.max(mpart_ref[h], axis=-1, keepdims=True), (tq, LANES)))
    acc_ref[...] = jnp.zeros_like(acc_ref)
    loop(pass2)
    for h in range(N_HEADS):
        acc = acc_ref[h]
        o_ref[0, :, h * HEAD_DIM:(h + 1) * HEAD_DIM] = (acc[:, 0:HEAD_DIM] / acc[:, HEAD_DIM:hw]).astype(BF16)


def _fox_call(l, fq, fk, fv, qg, kg, tq):
    b, s, wide = fq.shape
    assert s % tq == 0
    return pl.pallas_call(
        _fox_kernel,
        out_shape=jax.ShapeDtypeStruct((b, s, BRANCH_W), BF16),
        grid=(b, s // tq),
        in_specs=[
            pl.BlockSpec((1, tq, wide), lambda i, j: (i, j, 0)),
            pl.BlockSpec((1, s, wide), lambda i, j: (i, 0, 0)),
            pl.BlockSpec((1, s, wide), lambda i, j: (i, 0, 0)),
            _layer_spec(qg, l),
            _layer_spec(kg, l),
        ],
        out_specs=pl.BlockSpec((1, tq, BRANCH_W), lambda i, j: (i, j, 0)),
        scratch_shapes=[pltpu.VMEM((N_HEADS, tq, LANES), F32),
                        pltpu.VMEM((N_HEADS, tq, 2 * HEAD_DIM), BF16),
                        pltpu.VMEM((N_HEADS, tq, 2 * HEAD_DIM), F32)],
        compiler_params=_params("arbitrary", "arbitrary"),
        name="fox_attention",
    )(fq, fk, fv, qg, kg)


def _outproj_kernel(x_ref, mod_ref, g1_ref, ret_ref, gla_ref, fox_ref, wbr_ref, wmg_ref, bmg_ref, wo_ref,
                    o_ref):
    d = x_ref.shape[2]
    x = x_ref[0]
    mod = mod_ref[0, 0]
    hb = _norm_modulate(x, g1_ref[0], mod[:, 0:d], mod[:, d:2 * d]).astype(BF16)
    mixed = None
    for n, br_ref in enumerate((ret_ref, gla_ref, fox_ref)):
        y = _dot(br_ref[0], wbr_ref[0, n])
        gate = _sigmoid(_dot(hb, wmg_ref[0, :, n * d:(n + 1) * d]) + bmg_ref[0, :, n * d:(n + 1) * d])
        mixed = gate * y if mixed is None else mixed + gate * y
    out = _dot(mixed.astype(BF16), wo_ref[0])
    o_ref[0] = x + mod[:, 2 * d:3 * d] * out


def _outproj_call(l, x, mod4, g1, ret, gla, fox, wbr, wmg, bmg, wo):
    b, s, d = x.shape
    ts = min(TS_OUT, s)
    tile = lambda i, j: (i, j, 0)
    return pl.pallas_call(
        _outproj_kernel,
        out_shape=jax.ShapeDtypeStruct((b, s, d), F32),
        grid=(b, s // ts),
        in_specs=[
            pl.BlockSpec((1, ts, d), tile),
            pl.BlockSpec((1, 1, 1, mod4.shape[3]), lambda i, j: (l, i, 0, 0)),
            _layer_spec(g1, l),
            pl.BlockSpec((1, ts, BRANCH_W), tile),
            pl.BlockSpec((1, ts, BRANCH_W), tile),
            pl.BlockSpec((1, ts, BRANCH_W), tile),
            _layer_spec(wbr, l),
            _layer_spec(wmg, l),
            _layer_spec(bmg, l),
            _layer_spec(wo, l),
        ],
        out_specs=pl.BlockSpec((1, ts, d), tile),
        compiler_params=_params("arbitrary", "arbitrary"),
        name="out_proj",
    )(x, mod4, g1, ret, gla, fox, wbr, wmg, bmg, wo)


def _ffn_kernel(x_ref, mod_ref, g2_ref, wup_ref, wconv_ref, bconv_ref, wdown_ref, o_ref, tail_ref):
    d = x_ref.shape[2]
    ts = x_ref.shape[1]
    ff = wdown_ref.shape[1]

    @pl.when(pl.program_id(1) == 0)
    def _():
        tail_ref[...] = jnp.zeros_like(tail_ref)

    x = x_ref[0]
    mod = mod_ref[0, 0]
    hb = _norm_modulate(x, g2_ref[0], mod[:, 3 * d:4 * d], mod[:, 4 * d:5 * d]).astype(BF16)
    u = _dot(hb, wup_ref[0, :, 0:ff])
    g = _dot(hb, wup_ref[0, :, ff:2 * ff])
    row = lax.broadcasted_iota(jnp.int32, u.shape, 0)
    prev1 = tail_ref[CONV_W - 2:CONV_W - 1, :]
    prev2 = tail_ref[CONV_W - 3:CONV_W - 2, :]
    u1 = jnp.where(row == 0, prev1, pltpu.roll(u, 1, axis=0))
    u2 = jnp.where(row == 0, prev2, jnp.where(row == 1, prev1, pltpu.roll(u, 2, axis=0)))
    tail_ref[...] = u[ts - (CONV_W - 1):ts, :]
    wc = wconv_ref[0]
    conv = bconv_ref[0] + wc[0:1, :] * u2 + wc[1:2, :] * u1 + wc[2:3, :] * u
    act = (_silu(conv) * g).astype(BF16)
    y = _dot(act, wdown_ref[0])
    o_ref[0] = x + mod[:, 5 * d:6 * d] * y


def _ffn_call(l, x, mod4, g2, wup, wconv, bconv, wdown):
    b, s, d = x.shape
    ts = min(TS_FFN, s)
    ff = wdown.shape[1]
    tile = lambda i, j: (i, j, 0)
    return pl.pallas_call(
        _ffn_kernel,
        out_shape=jax.ShapeDtypeStruct((b, s, d), F32),
        grid=(b, s // ts),
        in_specs=[
            pl.BlockSpec((1, ts, d), tile),
            pl.BlockSpec((1, 1, 1, mod4.shape[3]), lambda i, j: (l, i, 0, 0)),
            _layer_spec(g2, l),
            _layer_spec(wup, l),
            _layer_spec(wconv, l),
            _layer_spec(bconv, l),
            _layer_spec(wdown, l),
        ],
        out_specs=pl.BlockSpec((1, ts, d), tile),
        scratch_shapes=[pltpu.VMEM((CONV_W - 1, ff), F32)],
        compiler_params=_params("arbitrary", "arbitrary"),
        name="conv_ffn",
    )(x, mod4, g2, wup, wconv, bconv, wdown)


def _rope_tables(s):
    half = HEAD_DIM // 2
    pos = jnp.arange(s, dtype=F32)
    inv_freq = ROPE_BASE ** (-jnp.arange(half, dtype=F32) / half)
    ang = pos[:, None] * inv_freq[None, :]
    cos, sin = jnp.cos(ang), jnp.sin(ang)
    return jnp.concatenate([cos, cos], axis=1), jnp.concatenate([-sin, sin], axis=1)


def _retention_tables(ts):
    log_g = jnp.log1p(-jnp.exp2(-5.0 - jnp.arange(N_HEADS, dtype=F32)))[:, None, None]
    t = jnp.arange(ts, dtype=F32)
    diff = t[:, None] - t[None, :]
    same = (jnp.arange(ts)[:, None] // CHUNK) == (jnp.arange(ts)[None, :] // CHUNK)
    expo = jnp.where(same, jnp.abs(diff), diff)[None]
    mask = jnp.where((same | (diff > 0))[None], jnp.exp(jnp.where(expo >= 0, expo, 0.0) * log_g), 0.0)
    ones = jnp.ones((1, 1, HEAD_DIM), F32)
    qw = jnp.exp((t + 1.0)[None, :, None] * log_g) * ones
    kw = jnp.exp((ts - 1.0 - t)[None, :, None] * log_g) * ones
    gts = jnp.exp(ts * log_g) * ones
    return mask, qw, kw, gts


def _chunk_matrices(ts):
    i = jnp.arange(ts)
    same = (i[:, None] // CHUNK) == (i[None, :] // CHUNK)
    btri = (same & (i[:, None] >= i[None, :])).astype(BF16)
    return btri, same.astype(BF16)


def kernel(x, c, norm1_g, norm2_g, w_ada, b_ada, w_in, w_gla_a2, b_gla_a, b_fox_f, ret_norm_g, gla_norm_g,
           q_norm_g, k_norm_g, w_br, w_mg, b_mg, w_o, w_up, w_conv, b_conv, w_down):
    b, s, d = x.shape
    depth = w_ada.shape[0]
    row = lambda a: a[:, None, :]

    mod4 = _ada_call(c, w_ada, b_ada)[:, :, None, :]
    wmain, wsmall = _win_prep_call(w_in.astype(BF16))
    wa2 = jnp.pad(w_gla_a2, ((0, 0), (0, LANES - GLA_LOWRANK), (0, 0))).astype(BF16)
    bfox = jnp.pad(b_fox_f, ((0, 0), (FF_LANE0, LANES - FF_LANE0 - N_HEADS)))
    wbr, wmg, wo, wup, wdown = (a.astype(BF16) for a in (w_br, w_mg, w_o, w_up, w_down))
    g1, g2, ba, bfox, rg, gg, qg, kg, bmg, bconv = (
        row(a) for a in (norm1_g, norm2_g, b_gla_a, bfox, ret_norm_g, gla_norm_g, q_norm_g, k_norm_g,
                         b_mg, b_conv))

    ts_in = min(TS_IN, s)
    ts_mix = min(TS_MIX, s)
    i = jnp.arange(ts_in)
    tri = (i[:, None] >= i[None, :]).astype(BF16)
    cos_t, sin_t = _rope_tables(s)
    mret, qw, kw, gts = _retention_tables(ts_mix)
    btri, bones = _chunk_matrices(ts_mix)

    for l in range(depth):
        r, gl, fq, fk, fv, la = _inproj_call(
            l, x, mod4, g1, wmain, wsmall, wa2, ba, bfox, cos_t, sin_t, qg, kg, tri)
        ret, gla_o = _mixer_call(l, r, gl, la, mret, qw, kw, gts, btri, bones, rg, gg)
        fox = _fox_call(l, fq, fk, fv, qg, kg, min(TQ_FOX, s))
        x = _outproj_call(l, x, mod4, g1, ret, gla_o, fox, wbr, wmg, bmg, wo)
        x = _ffn_call(l, x, mod4, g2, wup, w_conv, bconv, wdown)
    return x
```

```python
import jax
import jax.numpy as jnp
from jax import lax
from jax.experimental import pallas as pl
from jax.experimental.pallas import tpu as pltpu

F32 = jnp.float32
BF16 = jnp.bfloat16

CHUNK = 64
N_HEADS = 4
HEAD_DIM = 128
BRANCH_W = N_HEADS * HEAD_DIM
GLA_DK = HEAD_DIM // 2
GLA_LOWRANK = 16
GLA_TAU = 16.0
CONV_W = 3
ROPE_BASE = 10000.0
EPS = 1e-6

LANES = 128
FF_LANE0 = GLA_LOWRANK
AUG_T0 = 0
AUG_S0 = 3
AUG_ROWMAX0 = 6
LOG2E = 1.4426950408889634
FOX_Q_SCALE = HEAD_DIM ** -0.5 * LOG2E
NORM_SLACK = 1.02
BOUND_LIMIT = 40.0

VMEM_LIMIT = 56 * 1024 * 1024

TS_IN = 512
TS_MIX = 256
TQ_FOX = 512
TS_OUT = 512
TS_FFN = 512
TR_PREP = 256
_NK = N_HEADS * GLA_DK
O_GLR = 4 * BRANCH_W + 2 * _NK + BRANCH_W
O_GG = O_GLR + GLA_LOWRANK
O_FF = O_GG + 4 * BRANCH_W
W_MAIN = O_GLR + (O_FF - O_GG)


def _dot(a, b):
    return jnp.dot(a, b, preferred_element_type=F32)


def _dot_nt(a, b):
    return lax.dot_general(a, b, (((1,), (1,)), ((), ())), preferred_element_type=F32)


def _dot_tn(a, b):
    return lax.dot_general(a, b, (((0,), (0,)), ((), ())), preferred_element_type=F32)


def _split3(x):
    x_hi = x.astype(BF16).astype(F32)
    r1 = x - x_hi
    x_mid = r1.astype(BF16).astype(F32)
    x_lo = (r1 - x_mid).astype(BF16).astype(F32)
    return x_hi, x_mid, x_lo


def _split3_dot(m_bf16, x):
    x_hi, x_mid, x_lo = (t.astype(BF16) for t in _split3(x))
    return _dot(m_bf16, x_hi) + _dot(m_bf16, x_mid) + _dot(m_bf16, x_lo)


def _log_sigmoid(x):
    return jnp.minimum(x, 0.0) - jnp.log1p(jnp.exp(-jnp.abs(x)))


def _sigmoid(x):
    return 1.0 / (1.0 + jnp.exp(-x))


def _silu(x):
    return x * _sigmoid(x)


def _norm_modulate(x, g, shift, scale):
    ms = jnp.mean(x * x, axis=-1, keepdims=True)
    return x * lax.rsqrt(ms + EPS) * (g * (1.0 + scale)) + shift


def _fox_score_bound(qg, kg):
    gain = jnp.max(jnp.abs(qg), axis=-1, keepdims=True) * jnp.max(jnp.abs(kg), axis=-1, keepdims=True)
    return gain * (FOX_Q_SCALE * HEAD_DIM * NORM_SLACK)


def _params(*sem):
    return pltpu.CompilerParams(dimension_semantics=sem, vmem_limit_bytes=VMEM_LIMIT)


def _layer_spec(arr, l):
    zeros = (0,) * (arr.ndim - 1)
    return pl.BlockSpec((1,) + arr.shape[1:], lambda *_: (l,) + zeros, pipeline_mode=pl.Buffered(1))


def _const_spec(arr):
    zeros = (0,) * arr.ndim
    return pl.BlockSpec(arr.shape, lambda *_: zeros, pipeline_mode=pl.Buffered(1))


def _ada_kernel(c_ref, w_ref, b_ref, o_ref):
    c_act = _silu(c_ref[...]).astype(BF16)
    o_ref[0] = _dot(c_act, w_ref[0].astype(BF16)) + b_ref[0]


def _ada_call(c, w_ada, b_ada):
    depth, d, n = w_ada.shape
    b = c.shape[0]
    tn = n // 4
    return pl.pallas_call(
        _ada_kernel,
        out_shape=jax.ShapeDtypeStruct((depth, b, n), F32),
        grid=(depth, n // tn),
        in_specs=[
            pl.BlockSpec((b, d), lambda l, j: (0, 0)),
            pl.BlockSpec((1, d, tn), lambda l, j: (l, 0, j)),
            pl.BlockSpec((1, 1, tn), lambda l, j: (l, 0, j)),
        ],
        out_specs=pl.BlockSpec((1, b, tn), lambda l, j: (l, 0, j)),
        compiler_params=_params("arbitrary", "arbitrary"),
        name="ada_mod",
    )(c, w_ada, b_ada.reshape(depth, 1, n))


def _win_prep_kernel(w_ref, main_ref, small_ref):
    tr = w_ref.shape[1]
    main_ref[0, :, 0:O_GLR] = w_ref[0, :, 0:O_GLR].astype(BF16)
    main_ref[0, :, O_GLR:W_MAIN] = w_ref[0, :, O_GG:O_FF].astype(BF16)
    small = jnp.concatenate(
        [w_ref[0, :, O_GLR:O_GG], w_ref[0, :, O_FF:O_FF + N_HEADS],
         jnp.zeros((tr, LANES - GLA_LOWRANK - N_HEADS), w_ref.dtype)], axis=1)
    small_ref[0] = small.astype(BF16)


def _win_prep_call(w_in):
    depth, d, n = w_in.shape
    tr = TR_PREP
    tile = lambda l, i: (l, i, 0)
    return pl.pallas_call(
        _win_prep_kernel,
        out_shape=(jax.ShapeDtypeStruct((depth, d, W_MAIN), BF16),
                   jax.ShapeDtypeStruct((depth, d, LANES), BF16)),
        grid=(depth, d // tr),
        in_specs=[pl.BlockSpec((1, tr, n), tile)],
        out_specs=(pl.BlockSpec((1, tr, W_MAIN), tile), pl.BlockSpec((1, tr, LANES), tile)),
        compiler_params=_params("arbitrary", "arbitrary"),
        name="win_prep",
    )(w_in)


def _inproj_kernel(x_ref, mod_ref, g1_ref, wmain_ref, wsmall_ref, wa2_ref, ba_ref, bfox_ref,
                   cos_ref, sin_ref, qg_ref, kg_ref, tri_ref,
                   r_ref, gl_ref, fq_ref, fk_ref, fv_ref, la_ref, carry_ref):
    d = x_ref.shape[2]
    w = BRANCH_W

    @pl.when(pl.program_id(1) == 0)
    def _():
        carry_ref[...] = jnp.zeros_like(carry_ref)

    x = x_ref[0]
    mod = mod_ref[0, 0]
    h = _norm_modulate(x, g1_ref[0], mod[:, 0:d], mod[:, d:2 * d])
    hb = h.astype(BF16)
    cos = cos_ref[...]
    sin = sin_ref[...]

    def heads_map(acc, fn):
        return jnp.concatenate(
            [fn(acc[:, i * HEAD_DIM:(i + 1) * HEAD_DIM]) for i in range(N_HEADS)], axis=1)

    def rope(t):
        return t * cos + pltpu.roll(t, HEAD_DIM // 2, axis=1) * sin

    def qk_norm(g):
        def fn(t):
            ms = jnp.mean(t * t, axis=-1, keepdims=True)
            return t * lax.rsqrt(ms + EPS) * g
        return fn

    def proj(c0, n):
        return _dot(hb, wmain_ref[0, :, c0:c0 + n])

    small = _dot(hb, wsmall_ref[0])
    c0 = 7 * w
    fq, fk = proj(c0, w), proj(c0 + w, w)
    la_pre = _dot(small.astype(BF16), wa2_ref[0]) + ba_ref[0]
    la_ref[0] = _log_sigmoid(la_pre) * (1.0 / GLA_TAU)
    lane = lax.broadcasted_iota(jnp.int32, small.shape, 1)
    is_ff = (lane >= FF_LANE0) & (lane < FF_LANE0 + N_HEADS)
    log_f = jnp.where(is_ff, _log_sigmoid(small + bfox_ref[0]), 0.0)
    cum = _split3_dot(tri_ref[...], log_f) + carry_ref[...]
    carry_ref[...] = cum[cum.shape[0] - 1:cum.shape[0], :]
    rq, rk = proj(0, w), proj(w, w)

    q_norm, k_norm = qk_norm(qg_ref[0] * FOX_Q_SCALE), qk_norm(kg_ref[0])
    q_ones = jnp.where((lane >= AUG_S0) & (lane < AUG_S0 + 3), 1.0, 0.0)
    bound = _fox_score_bound(qg_ref[0], kg_ref[0])
    b_hi, b_mid, b_lo = _split3(jnp.broadcast_to(bound, (1, LANES)))
    lane1 = lax.broadcasted_iota(jnp.int32, (1, LANES), 1)
    ref_lanes = jnp.where(lane1 == AUG_ROWMAX0, -b_hi, jnp.where(
        lane1 == AUG_ROWMAX0 + 1, -b_mid, jnp.where(lane1 == AUG_ROWMAX0 + 2, -b_lo, 0.0)))
    q_ones = q_ones + jnp.where(bound < BOUND_LIMIT, ref_lanes, 0.0)
    k_ones = jnp.where(((lane >= AUG_T0) & (lane < AUG_T0 + 3))
                       | ((lane >= AUG_ROWMAX0) & (lane < AUG_ROWMAX0 + 3)), 1.0, 0.0)
    cum_terms = _split3(cum * LOG2E)
    for i in range(N_HEADS):
        sl = slice(i * HEAD_DIM, (i + 1) * HEAD_DIM)
        o = 2 * i * HEAD_DIM
        c_hi, c_mid, c_lo = (jnp.broadcast_to(t[:, FF_LANE0 + i:FF_LANE0 + i + 1], (cum.shape[0], LANES))
                             for t in cum_terms)
        q_extra = jnp.where(lane == AUG_T0, c_hi, jnp.where(lane == AUG_T0 + 1, c_mid, jnp.where(
            lane == AUG_T0 + 2, c_lo, q_ones)))
        k_extra = jnp.where(lane == AUG_S0, -c_hi, jnp.where(lane == AUG_S0 + 1, -c_mid, jnp.where(
            lane == AUG_S0 + 2, -c_lo, k_ones)))
        fq_ref[0, :, o:o + HEAD_DIM] = q_norm(fq[:, sl]).astype(BF16)
        fq_ref[0, :, o + HEAD_DIM:o + 2 * HEAD_DIM] = q_extra.astype(BF16)
        fk_ref[0, :, o:o + HEAD_DIM] = k_norm(fk[:, sl]).astype(BF16)
        fk_ref[0, :, o + HEAD_DIM:o + 2 * HEAD_DIM] = k_extra.astype(BF16)

    gqk = proj(4 * w, w)
    r_ref[0, :, 0:w] = heads_map(rq, rope).astype(BF16)
    r_ref[0, :, w:2 * w] = (heads_map(rk, rope) * HEAD_DIM ** -0.5).astype(BF16)
    gqk = jnp.where(lax.broadcasted_iota(jnp.int32, gqk.shape, 1) < _NK, gqk * GLA_DK ** -0.5, gqk)
    gl_ref[0, :, 0:w] = gqk.astype(BF16)
    r_ref[0, :, 3 * w:4 * w] = _silu(proj(3 * w, w)).astype(BF16)
    gl_ref[0, :, 2 * w:3 * w] = _silu(proj(6 * w, w)).astype(BF16)
    r_ref[0, :, 2 * w:3 * w] = proj(2 * w, w).astype(BF16)
    gl_ref[0, :, w:2 * w] = proj(5 * w, w).astype(BF16)
    fv = proj(c0 + 2 * w, w)
    for i in range(N_HEADS):
        o = 2 * i * HEAD_DIM
        fv_ref[0, :, o:o + HEAD_DIM] = fv[:, i * HEAD_DIM:(i + 1) * HEAD_DIM].astype(BF16)
        fv_ref[0, :, o + HEAD_DIM:o + 2 * HEAD_DIM] = jnp.ones((fv.shape[0], HEAD_DIM), BF16)


def _inproj_call(l, x, mod4, g1, wmain, wsmall, wa2, ba, bfox, cos_t, sin_t, qg, kg, tri):
    b, s, d = x.shape
    ts = min(TS_IN, s)
    w = BRANCH_W
    tile = lambda i, j: (i, j, 0)
    return pl.pallas_call(
        _inproj_kernel,
        out_shape=(
            jax.ShapeDtypeStruct((b, s, 4 * w), BF16),
            jax.ShapeDtypeStruct((b, s, 3 * w), BF16),
            jax.ShapeDtypeStruct((b, s, 2 * w), BF16),
            jax.ShapeDtypeStruct((b, s, 2 * w), BF16),
            jax.ShapeDtypeStruct((b, s, 2 * w), BF16),
            jax.ShapeDtypeStruct((b, s, _NK), F32),
        ),
        grid=(b, s // ts),
        in_specs=[
            pl.BlockSpec((1, ts, d), tile),
            pl.BlockSpec((1, 1, 1, mod4.shape[3]), lambda i, j: (l, i, 0, 0)),
            _layer_spec(g1, l),
            _layer_spec(wmain, l),
            _layer_spec(wsmall, l),
            _layer_spec(wa2, l),
            _layer_spec(ba, l),
            _layer_spec(bfox, l),
            pl.BlockSpec((ts, HEAD_DIM), lambda i, j: (j, 0)),
            pl.BlockSpec((ts, HEAD_DIM), lambda i, j: (j, 0)),
            _layer_spec(qg, l),
            _layer_spec(kg, l),
            _const_spec(tri),
        ],
        out_specs=(
            pl.BlockSpec((1, ts, 4 * w), tile),
            pl.BlockSpec((1, ts, 3 * w), tile),
            pl.BlockSpec((1, ts, 2 * w), tile),
            pl.BlockSpec((1, ts, 2 * w), tile),
            pl.BlockSpec((1, ts, 2 * w), tile),
            pl.BlockSpec((1, ts, _NK), tile),
        ),
        scratch_shapes=[pltpu.VMEM((1, LANES), F32)],
        compiler_params=_params("arbitrary", "arbitrary"),
        name="in_proj",
    )(x, mod4, g1, wmain, wsmall, wa2, ba, bfox, cos_t, sin_t, qg, kg, tri)


def _mixer_kernel(r_ref, gl_ref, la_ref, mret_ref, qw_ref, kw_ref, gts_ref, btri_ref, bones_ref,
                  rg_ref, gg_ref, ret_ref, gla_ref, rstate_ref, gstate_ref):
    w = BRANCH_W
    ts = r_ref.shape[1]

    @pl.when(pl.program_id(1) == 0)
    def _():
        rstate_ref[...] = jnp.zeros_like(rstate_ref)
        gstate_ref[...] = jnp.zeros_like(gstate_ref)

    def retention_head(h):
        sl = slice(h * HEAD_DIM, (h + 1) * HEAD_DIM)
        q = r_ref[0, :, sl]
        k = r_ref[0, :, w + h * HEAD_DIM:w + (h + 1) * HEAD_DIM]
        v = r_ref[0, :, 2 * w + h * HEAD_DIM:2 * w + (h + 1) * HEAD_DIM]
        gate = r_ref[0, :, 3 * w + h * HEAD_DIM:3 * w + (h + 1) * HEAD_DIM].astype(F32)
        state = rstate_ref[h]
        p = (_dot_nt(q, k) * mret_ref[h]).astype(BF16)
        o = _dot(p, v) + _dot(q, state.astype(BF16)) * qw_ref[h]
        kk = (k.astype(F32) * kw_ref[h]).astype(BF16)
        rstate_ref[h] = gts_ref[h] * state + _dot_tn(kk, v)
        mu = jnp.mean(o, axis=-1, keepdims=True)
        oc = o - mu
        var = jnp.mean(oc * oc, axis=-1, keepdims=True)
        on = oc * lax.rsqrt(var + EPS) * rg_ref[0, :, sl]
        ret_ref[0, :, sl] = (gate * on).astype(BF16)

    la = la_ref[0]
    b_cum = _split3_dot(btri_ref[...], la)
    b_end = _split3_dot(bones_ref[...], la)
    retention_head(0)
    kdec = jnp.exp(b_end - b_cum)
    a_all = jnp.exp(b_end)
    kk_all = (gl_ref[0, :, _NK:2 * _NK].astype(F32) * kdec).astype(BF16)
    q_all = gl_ref[0, :, 0:_NK]
    lane = lax.broadcasted_iota(jnp.int32, (ts, LANES), 1)
    chunks = [slice(c * CHUNK, (c + 1) * CHUNK) for c in range(ts // CHUNK)]
    pair_lanes = [slice((h // 2) * LANES, (h // 2 + 1) * LANES) for h in range(N_HEADS)]
    kvs = []
    for h in range(N_HEADS):
        kk = kk_all[:, pair_lanes[h]]
        v = gl_ref[0, :, w + h * HEAD_DIM:w + (h + 1) * HEAD_DIM]
        kvs.append([_dot_tn(v[rows], kk[rows]) for rows in chunks])
    retention_head(1)
    states = []
    for h in range(N_HEADS):
        state = gstate_ref[h]
        per_chunk = []
        for c, rows in enumerate(chunks):
            state = a_all[c * CHUNK:c * CHUNK + 1, pair_lanes[h]] * state + kvs[h][c]
            per_chunk.append(state.astype(BF16))
        gstate_ref[h] = state
        states.append(per_chunk)
    retention_head(2)
    for h in range(N_HEADS):
        if h == N_HEADS // 2:
            retention_head(3)
        own = (lane >= (h % 2) * GLA_DK) & (lane < (h % 2 + 1) * GLA_DK)
        qm = jnp.where(own, q_all[:, pair_lanes[h]], jnp.zeros((), BF16))
        gate = gl_ref[0, :, 2 * w + h * HEAD_DIM:2 * w + (h + 1) * HEAD_DIM].astype(F32)
        o = jnp.concatenate([_dot_nt(qm[rows], states[h][c]) for c, rows in enumerate(chunks)], axis=0)
        ms = jnp.mean(o * o, axis=-1, keepdims=True)
        on = o * lax.rsqrt(ms + EPS) * gg_ref[0]
        gla_ref[0, :, h * HEAD_DIM:(h + 1) * HEAD_DIM] = (gate * on).astype(BF16)


def _mixer_call(l, r, gl, la, mret, qw, kw, gts, btri, bones, rg, gg):
    b, s, _ = r.shape
    ts = mret.shape[1]
    w = BRANCH_W
    tile = lambda i, j: (i, j, 0)
    return pl.pallas_call(
        _mixer_kernel,
        out_shape=(jax.ShapeDtypeStruct((b, s, w), BF16), jax.ShapeDtypeStruct((b, s, w), BF16)),
        grid=(b, s // ts),
        in_specs=[
            pl.BlockSpec((1, ts, 4 * w), tile),
            pl.BlockSpec((1, ts, 3 * w), tile),
            pl.BlockSpec((1, ts, _NK), tile),
            _const_spec(mret),
            _const_spec(qw),
            _const_spec(kw),
            _const_spec(gts),
            _const_spec(btri),
            _const_spec(bones),
            _layer_spec(rg, l),
            _layer_spec(gg, l),
        ],
        out_specs=(pl.BlockSpec((1, ts, w), tile), pl.BlockSpec((1, ts, w), tile)),
        scratch_shapes=[pltpu.VMEM((N_HEADS, HEAD_DIM, HEAD_DIM), F32),
                        pltpu.VMEM((N_HEADS, HEAD_DIM, LANES), F32)],
        compiler_params=_params("arbitrary", "arbitrary"),
        name="ret_gla_mixer",
    )(r, gl, la, mret, qw, kw, gts, btri, bones, rg, gg)


def _fox_kernel(q_ref, k_ref, v_ref, qg_ref, kg_ref, o_ref, mpart_ref, q2_ref, acc_ref):
    tq = q_ref.shape[1]
    qi = pl.program_id(1)
    hw = 2 * HEAD_DIM
    hs = [slice(h * hw, (h + 1) * hw) for h in range(N_HEADS)]
    half = tq // 2
    whole, top, bottom = slice(0, tq), slice(0, half), slice(half, tq)
    top_mask = (lax.broadcasted_iota(jnp.int32, (half, half), 0)
                >= lax.broadcasted_iota(jnp.int32, (half, half), 1))
    bottom_mask = (lax.broadcasted_iota(jnp.int32, (half, tq), 1)
                   <= lax.broadcasted_iota(jnp.int32, (half, tq), 0) + half)

    def scores(lhs, keys, mask):
        out = [_dot_nt(lhs(h), k_ref[0, keys, hs[h]]) for h in range(N_HEADS)]
        if mask is not None:
            out = [jnp.where(mask, s, -jnp.inf) for s in out]
        return out

    def pass1(qrows, start, width, mask):
        keys = pl.ds(pl.multiple_of(start, width), width)
        for h, s in enumerate(scores(lambda h: q_ref[0, qrows, hs[h]], keys, mask)):
            m = mpart_ref[h, qrows]
            for t in range(width // LANES):
                m = jnp.maximum(m, s[:, t * LANES:(t + 1) * LANES])
            mpart_ref[h, qrows] = m

    def pass2(lhs):
        def fn(qrows, start, width, mask):
            keys = pl.ds(pl.multiple_of(start, width), width)
            ps = [jnp.exp2(s).astype(BF16) for s in scores(lambda h: lhs(h, qrows), keys, mask)]
            for h in range(N_HEADS):
                acc_ref[h, qrows] += _dot(ps[h], v_ref[0, keys, hs[h]])
        return fn

    def loop(fn):
        def body(jb, carry):
            fn(whole, jb * tq, tq, None)
            return carry
        lax.fori_loop(0, qi, body, 0)
        fn(top, qi * tq, half, top_mask)
        fn(bottom, qi * tq, tq, bottom_mask)

    bound_is_safe = jnp.max(_fox_score_bound(qg_ref[0], kg_ref[0])) < BOUND_LIMIT

    lane = lax.broadcasted_iota(jnp.int32, (tq, LANES), 1)

    def set_reference(h, m):
        m_hi, m_mid, m_lo = _split3(m)
        extra = q_ref[0, :, h * hw + HEAD_DIM:(h + 1) * hw].astype(F32)
        extra = jnp.where(lane == AUG_ROWMAX0, -m_hi, jnp.where(
            lane == AUG_ROWMAX0 + 1, -m_mid, jnp.where(lane == AUG_ROWMAX0 + 2, -m_lo, extra)))
        q2_ref[h, :, 0:HEAD_DIM] = q_ref[0, :, h * hw:h * hw + HEAD_DIM]
        q2_ref[h, :, HEAD_DIM:hw] = extra.astype(BF16)

    acc_ref[...] = jnp.zeros_like(acc_ref)

    @pl.when(bound_is_safe)
    def _():
        loop(pass2(lambda h, qrows: q_ref[0, qrows, hs[h]]))

    @pl.when(jnp.logical_not(bound_is_safe))
    def _():
        mpart_ref[...] = jnp.full(mpart_ref.shape, -jnp.inf, F32)
        loop(pass1)
        for h in range(N_HEADS):
            set_reference(h, jnp.broadcast_to(jnp.max(mpart_ref[h], axis=-1, keepdims=True), (tq, LANES)))
        loop(pass2(lambda h, qrows: q2_ref[h, qrows]))
    for h in range(N_HEADS):
        acc = acc_ref[h]
        o_ref[0, :, h * HEAD_DIM:(h + 1) * HEAD_DIM] = (acc[:, 0:HEAD_DIM] / acc[:, HEAD_DIM:hw]).astype(BF16)


def _fox_call(l, fq, fk, fv, qg, kg, tq):
    b, s, wide = fq.shape
    assert s % tq == 0
    return pl.pallas_call(
        _fox_kernel,
        out_shape=jax.ShapeDtypeStruct((b, s, BRANCH_W), BF16),
        grid=(b, s // tq),
        in_specs=[
            pl.BlockSpec((1, tq, wide), lambda i, j: (i, j, 0)),
            pl.BlockSpec((1, s, wide), lambda i, j: (i, 0, 0)),
            pl.BlockSpec((1, s, wide), lambda i, j: (i, 0, 0)),
            _layer_spec(qg, l),
            _layer_spec(kg, l),
        ],
        out_specs=pl.BlockSpec((1, tq, BRANCH_W), lambda i, j: (i, j, 0)),
        scratch_shapes=[pltpu.VMEM((N_HEADS, tq, LANES), F32),
                        pltpu.VMEM((N_HEADS, tq, 2 * HEAD_DIM), BF16),
                        pltpu.VMEM((N_HEADS, tq, 2 * HEAD_DIM), F32)],
        compiler_params=_params("arbitrary", "arbitrary"),
        name="fox_attention",
    )(fq, fk, fv, qg, kg)


def _outproj_kernel(x_ref, mod_ref, g1_ref, ret_ref, gla_ref, fox_ref, wbr_ref, wmg_ref, bmg_ref, wo_ref,
                    o_ref):
    d = x_ref.shape[2]
    x = x_ref[0]
    mod = mod_ref[0, 0]
    hb = _norm_modulate(x, g1_ref[0], mod[:, 0:d], mod[:, d:2 * d]).astype(BF16)
    mixed = None
    for n, br_ref in enumerate((ret_ref, gla_ref, fox_ref)):
        y = _dot(br_ref[0], wbr_ref[0, n])
        gate = _sigmoid(_dot(hb, wmg_ref[0, :, n * d:(n + 1) * d]) + bmg_ref[0, :, n * d:(n + 1) * d])
        mixed = gate * y if mixed is None else mixed + gate * y
    out = _dot(mixed.astype(BF16), wo_ref[0])
    o_ref[0] = x + mod[:, 2 * d:3 * d] * out


def _outproj_call(l, x, mod4, g1, ret, gla, fox, wbr, wmg, bmg, wo):
    b, s, d = x.shape
    ts = min(TS_OUT, s)
    tile = lambda i, j: (i, j, 0)
    return pl.pallas_call(
        _outproj_kernel,
        out_shape=jax.ShapeDtypeStruct((b, s, d), F32),
        grid=(b, s // ts),
        in_specs=[
            pl.BlockSpec((1, ts, d), tile),
            pl.BlockSpec((1, 1, 1, mod4.shape[3]), lambda i, j: (l, i, 0, 0)),
            _layer_spec(g1, l),
            pl.BlockSpec((1, ts, BRANCH_W), tile),
            pl.BlockSpec((1, ts, BRANCH_W), tile),
            pl.BlockSpec((1, ts, BRANCH_W), tile),
            _layer_spec(wbr, l),
            _layer_spec(wmg, l),
            _layer_spec(bmg, l),
            _layer_spec(wo, l),
        ],
        out_specs=pl.BlockSpec((1, ts, d), tile),
        compiler_params=_params("arbitrary", "arbitrary"),
        name="out_proj",
    )(x, mod4, g1, ret, gla, fox, wbr, wmg, bmg, wo)


def _ffn_kernel(x_ref, mod_ref, g2_ref, wup_ref, wconv_ref, bconv_ref, wdown_ref, o_ref, tail_ref):
    d = x_ref.shape[2]
    ts = x_ref.shape[1]
    ff = wdown_ref.shape[1]

    @pl.when(pl.program_id(1) == 0)
    def _():
        tail_ref[...] = jnp.zeros_like(tail_ref)

    x = x_ref[0]
    mod = mod_ref[0, 0]
    hb = _norm_modulate(x, g2_ref[0], mod[:, 3 * d:4 * d], mod[:, 4 * d:5 * d]).astype(BF16)
    u = _dot(hb, wup_ref[0, :, 0:ff])
    g = _dot(hb, wup_ref[0, :, ff:2 * ff])
    row = lax.broadcasted_iota(jnp.int32, u.shape, 0)
    prev1 = tail_ref[CONV_W - 2:CONV_W - 1, :]
    prev2 = tail_ref[CONV_W - 3:CONV_W - 2, :]
    u1 = jnp.where(row == 0, prev1, pltpu.roll(u, 1, axis=0))
    u2 = jnp.where(row == 0, prev2, jnp.where(row == 1, prev1, pltpu.roll(u, 2, axis=0)))
    tail_ref[...] = u[ts - (CONV_W - 1):ts, :]
    wc = wconv_ref[0]
    conv = bconv_ref[0] + wc[0:1, :] * u2 + wc[1:2, :] * u1 + wc[2:3, :] * u
    act = (_silu(conv) * g).astype(BF16)
    y = _dot(act, wdown_ref[0])
    o_ref[0] = x + mod[:, 5 * d:6 * d] * y


def _ffn_call(l, x, mod4, g2, wup, wconv, bconv, wdown):
    b, s, d = x.shape
    ts = min(TS_FFN, s)
    ff = wdown.shape[1]
    tile = lambda i, j: (i, j, 0)
    return pl.pallas_call(
        _ffn_kernel,
        out_shape=jax.ShapeDtypeStruct((b, s, d), F32),
        grid=(b, s // ts),
        in_specs=[
            pl.BlockSpec((1, ts, d), tile),
            pl.BlockSpec((1, 1, 1, mod4.shape[3]), lambda i, j: (l, i, 0, 0)),
            _layer_spec(g2, l),
            _layer_spec(wup, l),
            _layer_spec(wconv, l),
            _layer_spec(bconv, l),
            _layer_spec(wdown, l),
        ],
        out_specs=pl.BlockSpec((1, ts, d), tile),
        scratch_shapes=[pltpu.VMEM((CONV_W - 1, ff), F32)],
        compiler_params=_params("arbitrary", "arbitrary"),
        name="conv_ffn",
    )(x, mod4, g2, wup, wconv, bconv, wdown)


def _rope_tables(s):
    half = HEAD_DIM // 2
    pos = jnp.arange(s, dtype=F32)
    inv_freq = ROPE_BASE ** (-jnp.arange(half, dtype=F32) / half)
    ang = pos[:, None] * inv_freq[None, :]
    cos, sin = jnp.cos(ang), jnp.sin(ang)
    return jnp.concatenate([cos, cos], axis=1), jnp.concatenate([-sin, sin], axis=1)


def _retention_tables(ts):
    log_g = jnp.log1p(-jnp.exp2(-5.0 - jnp.arange(N_HEADS, dtype=F32)))[:, None, None]
    t = jnp.arange(ts, dtype=F32)
    diff = t[:, None] - t[None, :]
    same = (jnp.arange(ts)[:, None] // CHUNK) == (jnp.arange(ts)[None, :] // CHUNK)
    expo = jnp.where(same, jnp.abs(diff), diff)[None]
    mask = jnp.where((same | (diff > 0))[None], jnp.exp(jnp.where(expo >= 0, expo, 0.0) * log_g), 0.0)
    ones = jnp.ones((1, 1, HEAD_DIM), F32)
    qw = jnp.exp((t + 1.0)[None, :, None] * log_g) * ones
    kw = jnp.exp((ts - 1.0 - t)[None, :, None] * log_g) * ones
    gts = jnp.exp(ts * log_g) * ones
    return mask, qw, kw, gts


def _chunk_matrices(ts):
    i = jnp.arange(ts)
    same = (i[:, None] // CHUNK) == (i[None, :] // CHUNK)
    btri = (same & (i[:, None] >= i[None, :])).astype(BF16)
    return btri, same.astype(BF16)


def kernel(x, c, norm1_g, norm2_g, w_ada, b_ada, w_in, w_gla_a2, b_gla_a, b_fox_f, ret_norm_g, gla_norm_g,
           q_norm_g, k_norm_g, w_br, w_mg, b_mg, w_o, w_up, w_conv, b_conv, w_down):
    b, s, d = x.shape
    depth = w_ada.shape[0]
    row = lambda a: a[:, None, :]

    mod4 = _ada_call(c, w_ada, b_ada)[:, :, None, :]
    wmain, wsmall = _win_prep_call(w_in.astype(BF16))
    wa2 = jnp.pad(w_gla_a2, ((0, 0), (0, LANES - GLA_LOWRANK), (0, 0))).astype(BF16)
    bfox = jnp.pad(b_fox_f, ((0, 0), (FF_LANE0, LANES - FF_LANE0 - N_HEADS)))
    wbr, wmg, wo, wup, wdown = (a.astype(BF16) for a in (w_br, w_mg, w_o, w_up, w_down))
    g1, g2, ba, bfox, rg, gg, qg, kg, bmg, bconv = (
        row(a) for a in (norm1_g, norm2_g, b_gla_a, bfox, ret_norm_g, gla_norm_g, q_norm_g, k_norm_g,
                         b_mg, b_conv))

    ts_in = min(TS_IN, s)
    ts_mix = min(TS_MIX, s)
    i = jnp.arange(ts_in)
    tri = (i[:, None] >= i[None, :]).astype(BF16)
    cos_t, sin_t = _rope_tables(s)
    mret, qw, kw, gts = _retention_tables(ts_mix)
    btri, bones = _chunk_matrices(ts_mix)

    for l in range(depth):
        r, gl, fq, fk, fv, la = _inproj_call(
            l, x, mod4, g1, wmain, wsmall, wa2, ba, bfox, cos_t, sin_t, qg, kg, tri)
        ret, gla_o = _mixer_call(l, r, gl, la, mret, qw, kw, gts, btri, bones, rg, gg)
        fox = _fox_call(l, fq, fk, fv, qg, kg, min(TQ_FOX, s))
        x = _outproj_call(l, x, mod4, g1, ret, gla_o, fox, wbr, wmg, bmg, wo)
        x = _ffn_call(l, x, mod4, g2, wup, w_conv, bconv, wdown)
    return x
```
